```python
import math
import jax, jax.numpy as jnp
from jax import lax
import numpy as np

D_MODEL = 1024
BATCH = 2
SEQ = 8192
DEPTH = 1
DEC_BATCH = 16
DEC_SEQ = 64
PAST_LEN = 2048

CHUNK = 64
N_META = 16
Q_BLOCK = 128
HA = 4
DHA = 64
DVA = 2 * DHA
HB = 8
DHB = 64
ROT_DIM = DHA // 4
ROPE_THETA = 500000.0
N_EXPERTS = 32
TOP_K = 4
D_FF = D_MODEL
SWIGLU_ALPHA = 1.702
SWIGLU_LIMIT = 7.0
EPS = 1e-5
QA_W = HA * 2 * DHA
QB_W = HB * DHB
KA_W = HA * 2 * DHA
KB_W = HB * DHB
VA_W = HA * DVA
VB_W = HB * DHB
Q_W = QA_W + QB_W
KV_W = KA_W + KB_W + VA_W + VB_W
IN_W = Q_W + KV_W

kernel_name = 'hybrid_diff_stickbreak_moe_stream_step'


def rmsnorm(x, g):
    xf = x.astype(jnp.float32)
    y = xf * lax.rsqrt(jnp.mean(xf * xf, axis=-1, keepdims=True) + EPS)
    return (y * g.astype(jnp.float32)).astype(x.dtype)


def rotary(x, pos):
    half = ROT_DIM // 2
    inv_freq = ROPE_THETA ** (-jnp.arange(0, ROT_DIM, 2, dtype=jnp.float32) / ROT_DIM)
    ang = pos.astype(jnp.float32)[:, None] * inv_freq[None, :]
    cos = jnp.cos(ang)[:, None, None, :].astype(x.dtype)
    sin = jnp.sin(ang)[:, None, None, :].astype(x.dtype)
    x1 = x[..., :half]
    x2 = x[..., half:ROT_DIM]
    return jnp.concatenate([x1 * cos - x2 * sin, x2 * cos + x1 * sin, x[..., ROT_DIM:]], axis=-1)


def split_q(p):
    lead = p.shape[:-1]
    qa = p[..., :QA_W].reshape(lead + (HA, 2, DHA))
    qb = p[..., QA_W:Q_W].reshape(lead + (HB, DHB))
    return qa, qb


def split_kv(p):
    lead = p.shape[:-1]
    o1 = KA_W
    o2 = o1 + KB_W
    o3 = o2 + VA_W
    ka = p[..., :o1].reshape(lead + (HA, 2, DHA))
    kb = p[..., o1:o2].reshape(lead + (HB, DHB))
    va = p[..., o2:o3].reshape(lead + (HA, DVA))
    vb = p[..., o3:].reshape(lead + (HB, DHB))
    return ka, kb, va, vb


def diff_attention(q, k, v, mask, lam, g_subln, lam_init):
    s = jnp.einsum('bqhcd,bkhcd->bhcqk', q, k).astype(jnp.float32) * (DHA ** -0.5)
    p = jax.nn.softmax(jnp.where(mask, s, -jnp.inf), axis=-1)
    w = p[:, :, 0] - lam * p[:, :, 1]
    o = jnp.einsum('bhqk,bkhe->bqhe', w.astype(v.dtype), v)
    return rmsnorm(o, g_subln) * (1.0 - lam_init)


def stick_breaking(q, k, v, mask):
    z = jnp.einsum('bqhd,bkhd->bhqk', q, k).astype(jnp.float32) * (DHB ** -0.5)
    log_keep = jnp.where(mask, jax.nn.log_sigmoid(-z), 0.0)
    after = lax.cumsum(log_keep, axis=3, reverse=True) - log_keep
    a = jnp.where(mask, jnp.exp(jax.nn.log_sigmoid(z) + after), 0.0)
    return jnp.einsum('bhqk,bkhd->bqhd', a.astype(v.dtype), v)


def moe(h, w_router, b_router, w_up, b_up, w_down, b_down):
    logits = (h @ w_router).astype(jnp.float32) + b_router.astype(jnp.float32)
    top_val, top_idx = lax.top_k(logits, TOP_K)
    top_w = jax.nn.softmax(top_val, axis=-1)
    gates = jnp.sum(jax.nn.one_hot(top_idx, N_EXPERTS, dtype=jnp.float32) * top_w[..., None], axis=1)
    y = jnp.zeros(h.shape, jnp.float32)
    for e in range(N_EXPERTS):
        u = h @ w_up[e] + b_up[e]
        glu = jnp.minimum(u[:, 0::2], SWIGLU_LIMIT)
        lin = jnp.clip(u[:, 1::2], -SWIGLU_LIMIT, SWIGLU_LIMIT)
        act = glu * jax.nn.sigmoid(SWIGLU_ALPHA * glu) * (lin + 1.0)
        y = y + gates[:, e:e + 1] * (act @ w_down[e] + b_down[e]).astype(jnp.float32)
    return y.astype(h.dtype)


def merge_and_ffn(x, h, o_a, o_b, w_branch_a, w_branch_b, w_gate, b_gate, w_out, g_ffn,
                  w_router, b_router, w_up, b_up, w_down, b_down):
    gate = jax.nn.sigmoid(h @ w_gate + b_gate)
    merged = gate[:, :D_MODEL] * (o_a @ w_branch_a) + gate[:, D_MODEL:] * (o_b @ w_branch_b)
    x = x + merged @ w_out
    return x + moe(rmsnorm(x, g_ffn), w_router, b_router, w_up, b_up, w_down, b_down)


def setup_inputs(seed: int = 0) -> dict:
    key = jax.random.key(seed)
    ks = jax.random.split(key, 28)
    f32 = jnp.float32

    def nrm(k, shape, scale):
        return jax.random.normal(k, shape, f32) * scale

    return {
        'x_prompt': nrm(ks[0], (BATCH, SEQ, D_MODEL), 1.0),
        'x_sample': nrm(ks[1], (DEC_BATCH, DEC_SEQ, D_MODEL), 1.0),
        'cache_diff_k': nrm(ks[2], (DEC_BATCH, PAST_LEN, HA, 2, DHA), 1.0),
        'cache_diff_v': nrm(ks[3], (DEC_BATCH, PAST_LEN, HA, DVA), 1.0),
        'cache_sb_k': nrm(ks[4], (DEC_BATCH, PAST_LEN, HB, DHB), 1.0),
        'cache_sb_v': nrm(ks[5], (DEC_BATCH, PAST_LEN, HB, DHB), 1.0),
        'meta_tokens': nrm(ks[6], (N_META, D_MODEL), 1.0),
        'g_mix': 1.0 + nrm(ks[7], (D_MODEL,), 0.02),
        'w_in': nrm(ks[8], (D_MODEL, IN_W), D_MODEL ** -0.5),
        'lambda_q1': nrm(ks[9], (DHA,), 0.1),
        'lambda_k1': nrm(ks[10], (DHA,), 0.1),
        'lambda_q2': nrm(ks[11], (DHA,), 0.1),
        'lambda_k2': nrm(ks[12], (DHA,), 0.1),
        'g_subln': 1.0 + nrm(ks[13], (DVA,), 0.02),
        'w_branch_a': nrm(ks[14], (VA_W, D_MODEL), VA_W ** -0.5),
        'w_branch_b': nrm(ks[15], (VB_W, D_MODEL), VB_W ** -0.5),
        'w_gate': nrm(ks[16], (D_MODEL, 2 * D_MODEL), D_MODEL ** -0.5),
        'b_gate': nrm(ks[17], (2 * D_MODEL,), 0.1),
        'w_out': nrm(ks[18], (D_MODEL, D_MODEL), D_MODEL ** -0.5),
        'g_ffn': 1.0 + nrm(ks[19], (D_MODEL,), 0.02),
        'w_router': nrm(ks[20], (D_MODEL, N_EXPERTS), D_MODEL ** -0.5),
        'b_router': nrm(ks[21], (N_EXPERTS,), 0.01),
        'w_up': nrm(ks[22], (N_EXPERTS, D_MODEL, 2 * D_FF), D_MODEL ** -0.5),
        'b_up': nrm(ks[23], (N_EXPERTS, 2 * D_FF), 0.02),
        'w_down': nrm(ks[24], (N_EXPERTS, D_FF, D_MODEL), D_FF ** -0.5),
        'b_down': nrm(ks[25], (N_EXPERTS, D_MODEL), 0.02),
        'g_final': 1.0 + nrm(ks[26], (D_MODEL,), 0.02),
    }


def reference(x_prompt, x_sample, cache_diff_k, cache_diff_v, cache_sb_k, cache_sb_v,
              meta_tokens, g_mix, w_in, lambda_q1, lambda_k1, lambda_q2, lambda_k2, g_subln,
              w_branch_a, w_branch_b, w_gate, b_gate, w_out, g_ffn, w_router, b_router,
              w_up, b_up, w_down, b_down, g_final):
    b, n = x_prompt.shape[0], x_prompt.shape[1]
    bs, ns = x_sample.shape[0], x_sample.shape[1]
    past = cache_diff_k.shape[1]
    x_p, x_s = x_prompt, x_sample
    for layer_idx in range(DEPTH):
        lam_init = 0.8 - 0.6 * math.exp(-0.3 * layer_idx)
        lam = (jnp.exp(jnp.sum(lambda_q1 * lambda_k1).astype(jnp.float32))
               - jnp.exp(jnp.sum(lambda_q2 * lambda_k2).astype(jnp.float32)) + lam_init)

        mka, mkb, mva, mvb = split_kv(rmsnorm(meta_tokens, g_mix) @ w_in[:, Q_W:])
        mka = rotary(mka, jnp.arange(N_META))

        h_p = rmsnorm(x_p, g_mix)
        proj_p = h_p @ w_in
        qa_p, qb_p = split_q(proj_p[..., :Q_W])
        ka_p, kb_p, va_p, vb_p = split_kv(proj_p[..., Q_W:])
        pos_p = N_META + jnp.arange(n)
        qa_p = rotary(qa_p, pos_p)
        ka_p = rotary(ka_p, pos_p)
        ka_all = jnp.concatenate([jnp.broadcast_to(mka, (b,) + mka.shape), ka_p], axis=1)
        kb_all = jnp.concatenate([jnp.broadcast_to(mkb, (b,) + mkb.shape), kb_p], axis=1)
        va_all = jnp.concatenate([jnp.broadcast_to(mva, (b,) + mva.shape), va_p], axis=1)
        vb_all = jnp.concatenate([jnp.broadcast_to(mvb, (b,) + mvb.shape), vb_p], axis=1)
        oa_blocks = []
        ob_blocks = []
        for blk in range(n // Q_BLOCK):
            q0 = blk * Q_BLOCK
            q1 = q0 + Q_BLOCK
            k_end = N_META + q1
            qf = jnp.arange(q0, q1)
            kf = jnp.arange(-N_META, q1)
            chunk_mask = (kf[None, :] // CHUNK) <= (qf[:, None] // CHUNK)
            strict_mask = kf[None, :] < qf[:, None]
            oa_blocks.append(diff_attention(qa_p[:, q0:q1], ka_all[:, :k_end], va_all[:, :k_end],
                                            chunk_mask, lam, g_subln, lam_init))
            ob_blocks.append(stick_breaking(qb_p[:, q0:q1], kb_all[:, :k_end], vb_all[:, :k_end],
                                            strict_mask))
        oa_p = jnp.concatenate(oa_blocks, axis=1).reshape(b * n, VA_W)
        ob_p = jnp.concatenate(ob_blocks, axis=1).reshape(b * n, VB_W)

        h_s = rmsnorm(x_s, g_mix)
        proj_s = h_s @ w_in
        qa_s, qb_s = split_q(proj_s[..., :Q_W])
        ka_s, kb_s, va_s, vb_s = split_kv(proj_s[..., Q_W:])
        pos_s = N_META + past + jnp.arange(ns)
        qa_s = rotary(qa_s, pos_s)
        ka_s = rotary(ka_s, pos_s)
        ka_cat = jnp.concatenate([jnp.broadcast_to(mka, (bs,) + mka.shape), cache_diff_k, ka_s], axis=1)
        va_cat = jnp.concatenate([jnp.broadcast_to(mva, (bs,) + mva.shape), cache_diff_v, va_s], axis=1)
        kb_cat = jnp.concatenate([jnp.broadcast_to(mkb, (bs,) + mkb.shape), cache_sb_k, kb_s], axis=1)
        vb_cat = jnp.concatenate([jnp.broadcast_to(mvb, (bs,) + mvb.shape), cache_sb_v, vb_s], axis=1)
        qf_s = past + jnp.arange(ns)
        kf_s = jnp.arange(-N_META, past + ns)
        chunk_mask_s = (kf_s[None, :] // CHUNK) <= (qf_s[:, None] // CHUNK)
        strict_mask_s = kf_s[None, :] < qf_s[:, None]
        oa_s = diff_attention(qa_s, ka_cat, va_cat, chunk_mask_s, lam, g_subln, lam_init).reshape(bs * ns, VA_W)
        ob_s = stick_breaking(qb_s, kb_cat, vb_cat, strict_mask_s).reshape(bs * ns, VB_W)

        rows = merge_and_ffn(
            jnp.concatenate([x_p.reshape(b * n, D_MODEL), x_s.reshape(bs * ns, D_MODEL)], axis=0),
            jnp.concatenate([h_p.reshape(b * n, D_MODEL), h_s.reshape(bs * ns, D_MODEL)], axis=0),
            jnp.concatenate([oa_p, oa_s], axis=0),
            jnp.concatenate([ob_p, ob_s], axis=0),
            w_branch_a, w_branch_b, w_gate, b_gate, w_out, g_ffn,
            w_router, b_router, w_up, b_up, w_down, b_down)
        x_p = rows[:b * n].reshape(b, n, D_MODEL)
        x_s = rows[b * n:].reshape(bs, ns, D_MODEL)

        diff_k_prompt, diff_v_prompt, sb_k_prompt, sb_v_prompt = ka_all, va_all, kb_all, vb_all
        diff_k_sample, diff_v_sample, sb_k_sample, sb_v_sample = ka_s, va_s, kb_s, vb_s

    y_prompt = rmsnorm(x_p, g_final)
    y_sample = rmsnorm(x_s, g_final)
    return (y_prompt, y_sample, diff_k_prompt, diff_v_prompt, sb_k_prompt, sb_v_prompt,
            diff_k_sample, diff_v_sample, sb_k_sample, sb_v_sample)
```

```python
import functools
import math

import jax
import jax.numpy as jnp
from jax import lax
from jax.experimental import pallas as pl
from jax.experimental.pallas import tpu as pltpu

F32 = jnp.float32
BF16 = jnp.bfloat16

N_META = 16
CHUNK = 64
HA = 4
DHA = 64
DVA = 128
HB = 8
DHB = 64
ROT_DIM = 16
ROPE_THETA = 500000.0
N_EXPERTS = 32
TOP_K = 4
SWIGLU_ALPHA = 1.702
SWIGLU_LIMIT = 7.0
EPS = 1e-5
LAM_INIT = 0.8 - 0.6 * math.exp(-0.3 * 0)

LANES = 128
NEG_BIG = -1e30
SB_SEG = 256
VMEM_LIMIT = 56 * 1024 * 1024


def _cparams(sem):
    return pltpu.CompilerParams(dimension_semantics=sem, vmem_limit_bytes=VMEM_LIMIT)


def _nt_dot(a, b):
    return lax.dot_general(a, b, (((1,), (1,)), ((), ())), preferred_element_type=F32)


def _dot(a, b):
    return jnp.dot(a, b, preferred_element_type=F32)


def _proj_kernel(x_ref, g_ref, w_ref, cos_ref, sa_ref, sb_ref,
                 ka32_ref, kb32_ref, va32_ref, vb32_ref,
                 qa_ref, qb_ref, ka_ref, kb_ref, va_ref, vb_ref):
    x = x_ref[...]
    ms = jnp.mean(x * x, axis=-1, keepdims=True)
    h = (x * lax.rsqrt(ms + EPS) * g_ref[...]).astype(BF16)
    proj = _dot(h, w_ref[...])
    cos = cos_ref[...]
    sa = sa_ref[...]
    sb = sb_ref[...]
    ones = jnp.ones((x.shape[0], LANES), BF16)

    def rot(c):
        return c * cos + pltpu.roll(c, 8, 1) * sa + pltpu.roll(c, LANES - 8, 1) * sb

    for j in range(4):
        lo = j * LANES
        qa = rot(proj[:, lo:lo + LANES]) * (DHA ** -0.5)
        qa_ref[j] = qa.astype(BF16)
        qb = proj[:, 512 + lo:512 + lo + LANES] * (DHB ** -0.5)
        qb_ref[j] = qb.astype(BF16)
        ka = rot(proj[:, 1024 + lo:1024 + lo + LANES])
        ka32_ref[:, lo:lo + LANES] = ka
        ka_ref[j] = ka.astype(BF16)
        kb = proj[:, 1536 + lo:1536 + lo + LANES]
        kb32_ref[:, lo:lo + LANES] = kb
        kb_ref[j] = kb.astype(BF16)
        va = proj[:, 2048 + lo:2048 + lo + LANES]
        va32_ref[:, lo:lo + LANES] = va
        va_ref[j, :, :LANES] = va.astype(BF16)
        va_ref[j, :, LANES:] = ones
        vb = proj[:, 2560 + lo:2560 + lo + LANES]
        vb32_ref[:, lo:lo + LANES] = vb
        vb_ref[j] = vb.astype(BF16)


def _proj(x, g_mix, w_in_bf, tabs, tile, tab_index):
    rows, d = x.shape
    n_tiles = rows // tile
    row_spec = lambda w: pl.BlockSpec((tile, w), lambda i: (i, 0))
    head_spec = lambda w: pl.BlockSpec((4, tile, w), lambda i: (0, i, 0))
    tab_spec = pl.BlockSpec((tile, LANES), lambda i: (tab_index(i), 0))
    const = lambda shape: pl.BlockSpec(shape, lambda i: (0,) * len(shape))
    f32_rows = jax.ShapeDtypeStruct((rows, 512), F32)
    bf_heads = jax.ShapeDtypeStruct((4, rows, LANES), BF16)
    return pl.pallas_call(
        _proj_kernel,
        grid=(n_tiles,),
        in_specs=[row_spec(d), const((1, d)), const(w_in_bf.shape), tab_spec, tab_spec, tab_spec],
        out_specs=[row_spec(512)] * 4 + [head_spec(LANES)] * 4 + [head_spec(2 * LANES), head_spec(LANES)],
        out_shape=[f32_rows] * 4 + [bf_heads] * 4
        + [jax.ShapeDtypeStruct((4, rows, 2 * LANES), BF16), bf_heads],
        compiler_params=_cparams(("parallel",)),
        name="proj",
    )(x, g_mix.reshape(1, d), w_in_bf, *tabs)


def _rotary_tables(pos):
    half = ROT_DIM // 2
    inv_freq = ROPE_THETA ** (-jnp.arange(0, ROT_DIM, 2, dtype=F32) / ROT_DIM)
    ang = pos.astype(F32)[:, None] * inv_freq[None, :]
    cos, sin = jnp.cos(ang), jnp.sin(ang)
    lane = jnp.arange(LANES) % DHA
    idx = lane % half
    first = (lane < half)[None, :]
    second = ((lane >= half) & (lane < ROT_DIM))[None, :]
    cos_t = jnp.where(first | second, cos[:, idx], 1.0)
    sa_t = jnp.where(second, sin[:, idx], 0.0)
    sb_t = jnp.where(first, -sin[:, idx], 0.0)
    return cos_t, sa_t, sb_t


def _stack_masked(q, split):
    lane = lax.broadcasted_iota(jnp.int32, q.shape, 1)
    zero = jnp.zeros_like(q)
    return jnp.concatenate([jnp.where(lane < split, q, zero), jnp.where(lane >= split, q, zero)], axis=0)


def _diff_update(q2, k, vaug, m_ref, acc_ref, mask):
    s = _nt_dot(q2, k)
    if mask is not None:
        s = jnp.where(mask, s, NEG_BIG)
    m_prev = m_ref[...]
    m_new = jnp.maximum(m_prev, jnp.max(s, axis=-1, keepdims=True))
    alpha = jnp.exp(m_prev - m_new)
    p = jnp.exp(s - m_new[:, :1])
    pv = _dot(p.astype(BF16), vaug)
    acc_ref[...] = jnp.concatenate([alpha, alpha], axis=1) * acc_ref[...] + pv
    m_ref[...] = m_new


def _diff_finish(acc_ref, lam_ref, g_ref, t):
    lq1, lk1, lq2, lk2 = (lam_ref[i:i + 1, :] for i in range(4))
    lam = (jnp.exp(jnp.sum(lq1 * lk1, axis=-1, keepdims=True))
           - jnp.exp(jnp.sum(lq2 * lk2, axis=-1, keepdims=True)) + LAM_INIT)
    acc = acc_ref[...]
    o0 = acc[:t, :LANES] / acc[:t, LANES:]
    o1 = acc[t:, :LANES] / acc[t:, LANES:]
    o = o0 - lam * o1
    ms = jnp.mean(o * o, axis=-1, keepdims=True)
    return o * lax.rsqrt(ms + EPS) * g_ref[...] * (1.0 - LAM_INIT)


def _tri_incl(n):
    r = lax.broadcasted_iota(jnp.int32, (n, n), 0)
    c = lax.broadcasted_iota(jnp.int32, (n, n), 1)
    return jnp.where(r >= c, 1.0, 0.0).astype(BF16)


def _sb_update(q2, k, v, tri, carry_ref, acc_ref, mask):
    z = _nt_dot(q2, k)
    nlk = jnp.maximum(z, 0.0) + jnp.log1p(jnp.exp(-jnp.abs(z)))
    if mask is not None:
        nlk = jnp.where(mask, nlk, 0.0)
    hi = nlk.astype(BF16)
    lo = (nlk - hi.astype(F32)).astype(BF16)
    incl = _dot(hi, tri) + _dot(lo, tri)
    carry = carry_ref[...]
    a = jnp.exp(z - incl - carry[:, :1])
    if mask is not None:
        a = jnp.where(mask, a, 0.0)
    acc_ref[...] += _dot(a.astype(BF16), v)
    carry_ref[...] = carry + incl[:, :1]


def _sb_finish(acc_ref, t):
    acc = acc_ref[...]
    lane = lax.broadcasted_iota(jnp.int32, (t, LANES), 1)
    return jnp.where(lane < DHB, acc[:t], acc[t:])


def _diff_prompt_kernel(q_ref, k_ref, v_ref, mk_ref, mv_ref, lam_ref, g_ref, o_ref,
                        m_ref, acc_ref, *, tq, tk):
    i = pl.program_id(2)
    q2 = _stack_masked(q_ref[0], DHA)
    m_ref[...] = jnp.full(m_ref.shape, NEG_BIG, F32)
    acc_ref[...] = jnp.zeros(acc_ref.shape, F32)
    _diff_update(q2, mk_ref[0], mv_ref[0], m_ref, acc_ref, None)

    per_q = tq // tk

    def body(j, c):
        start = pl.multiple_of(j * tk, tk)
        _diff_update(q2, k_ref[0, pl.ds(start, tk), :], v_ref[0, pl.ds(start, tk), :], m_ref, acc_ref, None)
        return c

    lax.fori_loop(0, i * per_q, body, 0)

    row = lax.broadcasted_iota(jnp.int32, (tq, tk), 0)
    col = lax.broadcasted_iota(jnp.int32, (tq, tk), 1)
    for d in range(per_q):
        mask = ((col + d * tk) // CHUNK) <= (row // CHUNK)
        mask2 = jnp.concatenate([mask, mask], axis=0)
        start = pl.multiple_of((i * per_q + d) * tk, tk)
        _diff_update(q2, k_ref[0, pl.ds(start, tk), :], v_ref[0, pl.ds(start, tk), :], m_ref, acc_ref, mask2)

    o_ref[...] = _diff_finish(acc_ref, lam_ref, g_ref, tq).astype(o_ref.dtype)


def _diff_prompt(qa, ka, va, mka, mva, lams, g_subln, batch, seq, tq, tk):
    nq = seq // tq
    kern = functools.partial(_diff_prompt_kernel, tq=tq, tk=tk)
    return pl.pallas_call(
        kern,
        grid=(batch, HA, nq),
        in_specs=[
            pl.BlockSpec((1, tq, LANES), lambda b, h, i: (h, b * nq + i, 0)),
            pl.BlockSpec((1, seq, LANES), lambda b, h, i: (h, b, 0)),
            pl.BlockSpec((1, seq, 2 * LANES), lambda b, h, i: (h, b, 0)),
            pl.BlockSpec((1, N_META, LANES), lambda b, h, i: (h, 0, 0)),
            pl.BlockSpec((1, N_META, 2 * LANES), lambda b, h, i: (h, 0, 0)),
            pl.BlockSpec((4, DHA), lambda b, h, i: (0, 0)),
            pl.BlockSpec((1, DVA), lambda b, h, i: (0, 0)),
        ],
        out_specs=pl.BlockSpec((tq, LANES), lambda b, h, i: (b * nq + i, h)),
        out_shape=jax.ShapeDtypeStruct((batch * seq, HA * DVA), BF16),
        scratch_shapes=[pltpu.VMEM((2 * tq, LANES), F32), pltpu.VMEM((2 * tq, 2 * LANES), F32)],
        compiler_params=_cparams(("parallel", "parallel", "arbitrary")),
        name="diff_prompt",
    )(qa, ka, va, mka, mva, lams, g_subln.reshape(1, DVA))


def _sb_prompt_kernel(q_ref, k_ref, v_ref, mk_ref, mv_ref, o_ref, carry_ref, acc_ref, *, tq):
    i = pl.program_id(2)
    seg = SB_SEG
    per_q = tq // seg
    q2 = _stack_masked(q_ref[0], DHB)
    tri = _tri_incl(seg)
    carry_ref[...] = jnp.zeros(carry_ref.shape, F32)
    acc_ref[...] = jnp.zeros(acc_ref.shape, F32)

    row = lax.broadcasted_iota(jnp.int32, (tq, seg), 0)
    col = lax.broadcasted_iota(jnp.int32, (tq, seg), 1)
    for d in reversed(range(per_q)):
        mask = (col + d * seg) < row
        mask2 = jnp.concatenate([mask, mask], axis=0)
        start = pl.multiple_of((i * per_q + d) * seg, seg)
        _sb_update(q2, k_ref[0, pl.ds(start, seg), :], v_ref[0, pl.ds(start, seg), :], tri,
                   carry_ref, acc_ref, mask2)

    n_full = i * per_q

    def body(t, c):
        start = pl.multiple_of((n_full - 1 - t) * seg, seg)
        _sb_update(q2, k_ref[0, pl.ds(start, seg), :], v_ref[0, pl.ds(start, seg), :], tri,
                   carry_ref, acc_ref, None)
        return c

    lax.fori_loop(0, n_full, body, 0)
    _sb_update(q2, mk_ref[0], mv_ref[0], tri[:N_META, :N_META], carry_ref, acc_ref, None)
    o_ref[...] = _sb_finish(acc_ref, tq).astype(o_ref.dtype)


def _sb_prompt(qb, kb, vb, mkb, mvb, batch, seq, tq):
    nq = seq // tq
    kern = functools.partial(_sb_prompt_kernel, tq=tq)
    return pl.pallas_call(
        kern,
        grid=(batch, HB // 2, nq),
        in_specs=[
            pl.BlockSpec((1, tq, LANES), lambda b, h, i: (h, b * nq + i, 0)),
            pl.BlockSpec((1, seq, LANES), lambda b, h, i: (h, b, 0)),
            pl.BlockSpec((1, seq, LANES), lambda b, h, i: (h, b, 0)),
            pl.BlockSpec((1, N_META, LANES), lambda b, h, i: (h, 0, 0)),
            pl.BlockSpec((1, N_META, LANES), lambda b, h, i: (h, 0, 0)),
        ],
        out_specs=pl.BlockSpec((tq, LANES), lambda b, h, i: (b * nq + i, h)),
        out_shape=jax.ShapeDtypeStruct((batch * seq, HB * DHB), BF16),
        scratch_shapes=[pltpu.VMEM((2 * tq, LANES), F32), pltpu.VMEM((2 * tq, LANES), F32)],
        compiler_params=_cparams(("parallel", "parallel", "arbitrary")),
        name="sb_prompt",
    )(qb, kb, vb, mkb, mvb)


def _sample_kernel(qa_ref, qb_ref, nka_ref, nva_ref, nkb_ref, nvb_ref,
                   cka_ref, cva_ref, ckb_ref, cvb_ref,
                   mka_ref, mva_ref, mkb_ref, mvb_ref, lam_ref, g_ref,
                   oa_ref, ob_ref, m_ref, acca_ref, carry_ref, accb_ref, *, ns, past):
    seg = SB_SEG
    n_seg = past // seg
    row = lax.broadcasted_iota(jnp.int32, (ns, ns), 0)
    col = lax.broadcasted_iota(jnp.int32, (ns, ns), 1)

    q2 = _stack_masked(qa_ref[0], DHA)
    m_ref[...] = jnp.full(m_ref.shape, NEG_BIG, F32)
    acca_ref[...] = jnp.zeros(acca_ref.shape, F32)
    _diff_update(q2, mka_ref[0], mva_ref[0], m_ref, acca_ref, None)
    ones = jnp.ones((seg, LANES), BF16)
    for s in range(n_seg):
        k = cka_ref[0, s * seg:(s + 1) * seg, :].astype(BF16)
        v = cva_ref[0, s * seg:(s + 1) * seg, :].astype(BF16)
        _diff_update(q2, k, jnp.concatenate([v, ones], axis=1), m_ref, acca_ref, None)
    mask = ((past + col) // CHUNK) <= ((past + row) // CHUNK)
    _diff_update(q2, nka_ref[0], nva_ref[0], m_ref, acca_ref, jnp.concatenate([mask, mask], axis=0))
    oa_ref[...] = _diff_finish(acca_ref, lam_ref, g_ref, ns).astype(oa_ref.dtype)

    q2 = _stack_masked(qb_ref[0], DHB)
    tri = _tri_incl(seg)
    carry_ref[...] = jnp.zeros(carry_ref.shape, F32)
    accb_ref[...] = jnp.zeros(accb_ref.shape, F32)
    strict = col < row
    _sb_update(q2, nkb_ref[0], nvb_ref[0], tri[:ns, :ns], carry_ref, accb_ref,
               jnp.concatenate([strict, strict], axis=0))
    for s in reversed(range(n_seg)):
        k = ckb_ref[0, s * seg:(s + 1) * seg, :].astype(BF16)
        v = cvb_ref[0, s * seg:(s + 1) * seg, :].astype(BF16)
        _sb_update(q2, k, v, tri, carry_ref, accb_ref, None)
    _sb_update(q2, mkb_ref[0], mvb_ref[0], tri[:N_META, :N_META], carry_ref, accb_ref, None)
    ob_ref[...] = _sb_finish(accb_ref, ns).astype(ob_ref.dtype)


def _sample_attention(qa, qb, ka, va, kb, vb, cka, cva, ckb, cvb, mka, mva, mkb, mvb, lams, g_subln,
                      row0, nb, ns, past):
    blk0 = row0 // ns
    new = lambda w: pl.BlockSpec((1, ns, w), lambda b, h: (h, blk0 + b, 0))
    cache = pl.BlockSpec((1, past, LANES), lambda b, h: (b, 0, h))
    meta = lambda w: pl.BlockSpec((1, N_META, w), lambda b, h: (h, 0, 0))
    out = pl.BlockSpec((ns, LANES), lambda b, h: (b, h))
    kern = functools.partial(_sample_kernel, ns=ns, past=past)
    return pl.pallas_call(
        kern,
        grid=(nb, 4),
        in_specs=[new(LANES), new(LANES), new(LANES), new(2 * LANES), new(LANES), new(LANES),
                  cache, cache, cache, cache,
                  meta(LANES), meta(2 * LANES), meta(LANES), meta(LANES),
                  pl.BlockSpec((4, DHA), lambda b, h: (0, 0)),
                  pl.BlockSpec((1, DVA), lambda b, h: (0, 0))],
        out_specs=[out, out],
        out_shape=[jax.ShapeDtypeStruct((nb * ns, 512), BF16)] * 2,
        scratch_shapes=[pltpu.VMEM((2 * ns, LANES), F32), pltpu.VMEM((2 * ns, 2 * LANES), F32),
                        pltpu.VMEM((2 * ns, LANES), F32), pltpu.VMEM((2 * ns, LANES), F32)],
        compiler_params=_cparams(("parallel", "parallel")),
        name="sample_attn",
    )(qa, qb, ka, va, kb, vb, cka, cva, ckb, cvb, mka, mva, mkb, mvb, lams, g_subln.reshape(1, DVA))


def _merge_kernel(x_ref, oa_ref, ob_ref, gmix_ref, wg_ref, bg_ref, wa_ref, wb_ref, wo_ref,
                  gffn_ref, wr_ref, br_ref, x2_ref, h2_ref, gates_ref):
    x = x_ref[...]
    d = x.shape[1]
    ms = jnp.mean(x * x, axis=-1, keepdims=True)
    h = (x * lax.rsqrt(ms + EPS) * gmix_ref[...]).astype(BF16)
    gate = jax.nn.sigmoid(_dot(h, wg_ref[...]) + bg_ref[...])
    merged = gate[:, :d] * _dot(oa_ref[...], wa_ref[...]) + gate[:, d:] * _dot(ob_ref[...], wb_ref[...])
    x2 = x + _dot(merged.astype(BF16), wo_ref[...])
    x2_ref[...] = x2
    ms2 = jnp.mean(x2 * x2, axis=-1, keepdims=True)
    h2 = x2 * lax.rsqrt(ms2 + EPS) * gffn_ref[...]
    h2_ref[...] = h2.astype(BF16)

    h_hi = h2.astype(BF16)
    h_lo = (h2 - h_hi.astype(F32)).astype(BF16)
    wr = wr_ref[...]
    w_hi = wr.astype(BF16)
    w_lo = (wr - w_hi.astype(F32)).astype(BF16)
    logits = _dot(h_hi, w_hi) + _dot(h_lo, w_hi) + _dot(h_hi, w_lo) + br_ref[...]

    lane = lax.broadcasted_iota(jnp.int32, logits.shape, 1)
    work = logits
    picks = []
    for _ in range(TOP_K):
        top = jnp.max(work, axis=-1, keepdims=True)
        first = jnp.min(jnp.where(work == top, lane, N_EXPERTS), axis=-1, keepdims=True)
        hit = lane == first
        picks.append((top, hit))
        work = jnp.where(hit, -jnp.inf, work)
    exps = [jnp.exp(v - picks[0][0]) for v, _ in picks]
    denom = exps[0] + exps[1] + exps[2] + exps[3]
    gates = jnp.zeros(logits.shape, F32)
    for e, (_, hit) in zip(exps, picks):
        gates = gates + jnp.where(hit, e / denom, 0.0)
    gates_ref[...] = gates


def _merge(x, oa, ob, g_mix, w_gate, b_gate, w_a, w_b, w_out, g_ffn, w_router, b_router, tile):
    rows, d = x.shape
    row = lambda w: pl.BlockSpec((tile, w), lambda i: (i, 0))
    const = lambda shape: pl.BlockSpec(shape, lambda i: (0,) * len(shape))
    return pl.pallas_call(
        _merge_kernel,
        grid=(rows // tile,),
        in_specs=[row(d), row(512), row(512), const((1, d)), const((d, 2 * d)), const((1, 2 * d)),
                  const((512, d)), const((512, d)), const((d, d)), const((1, d)),
                  const((d, N_EXPERTS)), const((1, N_EXPERTS))],
        out_specs=[row(d), row(d), row(N_EXPERTS)],
        out_shape=[jax.ShapeDtypeStruct((rows, d), F32), jax.ShapeDtypeStruct((rows, d), BF16),
                   jax.ShapeDtypeStruct((rows, N_EXPERTS), F32)],
        compiler_params=_cparams(("parallel",)),
        name="merge",
    )(x, oa, ob, g_mix.reshape(1, d), w_gate.astype(BF16), b_gate.reshape(1, 2 * d),
      w_a.astype(BF16), w_b.astype(BF16), w_out.astype(BF16), g_ffn.reshape(1, d),
      w_router, b_router.reshape(1, N_EXPERTS))


def _moe_kernel(h_ref, gates_ref, x2_ref, wg_ref, wl_ref, bg_ref, bl_ref, wd_ref, bd_ref, gf_ref,
                y_ref, acc_ref):
    e = pl.program_id(1)

    @pl.when(e == 0)
    def _():
        acc_ref[...] = jnp.zeros(acc_ref.shape, F32)

    h = h_ref[...]
    glu = jnp.minimum(_dot(h, wg_ref[0]) + bg_ref[0], SWIGLU_LIMIT)
    lin = jnp.clip(_dot(h, wl_ref[0]) + bl_ref[0], -SWIGLU_LIMIT, SWIGLU_LIMIT)
    act = glu * jax.nn.sigmoid(SWIGLU_ALPHA * glu) * (lin + 1.0)
    y = _dot(act.astype(BF16), wd_ref[0]) + bd_ref[0]
    gates = gates_ref[...]
    lane = lax.broadcasted_iota(jnp.int32, gates.shape, 1)
    g_e = jnp.sum(jnp.where(lane == e, gates, 0.0), axis=-1, keepdims=True)
    acc_ref[...] += g_e * y

    @pl.when(e == pl.num_programs(1) - 1)
    def _():
        x3 = x2_ref[...] + acc_ref[...]
        ms = jnp.mean(x3 * x3, axis=-1, keepdims=True)
        y_ref[...] = x3 * lax.rsqrt(ms + EPS) * gf_ref[...]


def _moe(h2, gates, x2, w_glu, w_lin, b_glu, b_lin, w_down, b_down, g_final, tile):
    rows, d = x2.shape
    dff = w_glu.shape[2]
    row = lambda w: pl.BlockSpec((tile, w), lambda i, e: (i, 0))
    exp = lambda a, b: pl.BlockSpec((1, a, b), lambda i, e: (e, 0, 0))
    return pl.pallas_call(
        _moe_kernel,
        grid=(rows // tile, N_EXPERTS),
        in_specs=[row(d), row(N_EXPERTS), row(d), exp(d, dff), exp(d, dff), exp(1, dff), exp(1, dff),
                  exp(dff, d), exp(1, d), pl.BlockSpec((1, d), lambda i, e: (0, 0))],
        out_specs=row(d),
        out_shape=jax.ShapeDtypeStruct((rows, d), F32),
        scratch_shapes=[pltpu.VMEM((tile, d), F32)],
        compiler_params=_cparams(("parallel", "arbitrary")),
        name="moe",
    )(h2, gates, x2, w_glu, w_lin, b_glu, b_lin, w_down, b_down, g_final.reshape(1, d))


def _pick(n, pref):
    t = min(n, pref)
    assert n % t == 0, (n, pref)
    return t


def kernel(x_prompt, x_sample, cache_diff_k, cache_diff_v, cache_sb_k, cache_sb_v, meta_tokens, g_mix, w_in,
           lambda_q1, lambda_k1, lambda_q2, lambda_k2, g_subln, w_branch_a, w_branch_b, w_gate, b_gate, w_out,
           g_ffn, w_router, b_router, w_up, b_up, w_down, b_down, g_final):
    b, n, d = x_prompt.shape
    bs, ns, _ = x_sample.shape
    past = cache_diff_k.shape[1]
    rows_p, rows_s = b * n, bs * ns
    rows = rows_p + rows_s
    assert n % SB_SEG == 0 and past % SB_SEG == 0 and rows_p % ns == 0

    tile = _pick(math.gcd(n, rows_s), 512)
    x_all = jnp.concatenate([x_prompt.reshape(rows_p, d), x_sample.reshape(rows_s, d)], axis=0)
    w_in_bf = w_in.astype(BF16)

    pos = jnp.concatenate([N_META + jnp.arange(n), N_META + past + (jnp.arange(tile) % ns)])
    tabs = _rotary_tables(pos)
    n_ptiles, per_seq = rows_p // tile, n // tile
    tab_index = lambda i: jnp.where(i < n_ptiles, i % per_seq, per_seq)
    ka32, kb32, va32, vb32, qa, qb, ka, kb, va, vb = _proj(x_all, g_mix, w_in_bf, tabs, tile, tab_index)

    mtabs = _rotary_tables(jnp.arange(N_META))
    mka32, mkb32, mva32, mvb32, _, _, mka, mkb, mva, mvb = _proj(
        meta_tokens, g_mix, w_in_bf, mtabs, N_META, lambda i: i)

    lams = jnp.stack([lambda_q1, lambda_k1, lambda_q2, lambda_k2]).astype(F32)

    tq = _pick(n, 512)
    oa_p = _diff_prompt(qa, ka, va, mka, mva, lams, g_subln, b, n, tq, _pick(tq, 512))
    ob_p = _sb_prompt(qb, kb, vb, mkb, mvb, b, n, tq)
    oa_s, ob_s = _sample_attention(
        qa, qb, ka, va, kb, vb,
        cache_diff_k.reshape(bs, past, 512), cache_diff_v.reshape(bs, past, 512),
        cache_sb_k.reshape(bs, past, 512), cache_sb_v.reshape(bs, past, 512),
        mka, mva, mkb, mvb, lams, g_subln, rows_p, bs, ns, past)
    oa = jnp.concatenate([oa_p, oa_s], axis=0)
    ob = jnp.concatenate([ob_p, ob_s], axis=0)

    x2, h2, gates = _merge(x_all, oa, ob, g_mix, w_gate, b_gate, w_branch_a, w_branch_b, w_out, g_ffn,
                           w_router, b_router, tile)

    dff = w_down.shape[1]
    w_glu = w_up[:, :, 0::2].astype(BF16)
    w_lin = w_up[:, :, 1::2].astype(BF16)
    b_glu = b_up[:, 0::2].reshape(N_EXPERTS, 1, dff)
    b_lin = b_up[:, 1::2].reshape(N_EXPERTS, 1, dff)
    y = _moe(h2, gates, x2, w_glu, w_lin, b_glu, b_lin, w_down.astype(BF16),
             b_down.reshape(N_EXPERTS, 1, d), g_final, _pick(math.gcd(rows_p, rows_s), 1024))

    def with_meta(m32, t32, shape):
        full = jnp.concatenate([jnp.broadcast_to(m32[None], (b, N_META, 512)), t32[:rows_p].reshape(b, n, 512)], axis=1)
        return full.reshape((b, N_META + n) + shape)

    return (y[:rows_p].reshape(b, n, d), y[rows_p:].reshape(bs, ns, d),
            with_meta(mka32, ka32, (HA, 2, DHA)), with_meta(mva32, va32, (HA, DVA)),
            with_meta(mkb32, kb32, (HB, DHB)), with_meta(mvb32, vb32, (HB, DHB)),
            ka32[rows_p:].reshape(bs, ns, HA, 2, DHA), va32[rows_p:].reshape(bs, ns, HA, DVA),
            kb32[rows_p:].reshape(bs, ns, HB, DHB), vb32[rows_p:].reshape(bs, ns, HB, DHB))
```

```python
import functools
import math

import jax
import jax.numpy as jnp
from jax import lax
from jax.experimental import pallas as pl
from jax.experimental.pallas import tpu as pltpu

F32 = jnp.float32
BF16 = jnp.bfloat16

N_META = 16
CHUNK = 64
HA = 4
DHA = 64
DVA = 128
HB = 8
DHB = 64
ROT_DIM = 16
ROPE_THETA = 500000.0
N_EXPERTS = 32
TOP_K = 4
SWIGLU_ALPHA = 1.702
SWIGLU_LIMIT = 7.0
EPS = 1e-5
LAM_INIT = 0.8 - 0.6 * math.exp(-0.3 * 0)

LANES = 128
NEG_BIG = -1e30
SB_SEG = 256
SB_DEAD = 110.0
VMEM_LIMIT = 56 * 1024 * 1024
MOE_CHUNK = 256


def _cparams(sem):
    return pltpu.CompilerParams(dimension_semantics=sem, vmem_limit_bytes=VMEM_LIMIT)


def _nt_dot(a, b):
    return lax.dot_general(a, b, (((1,), (1,)), ((), ())), preferred_element_type=F32)


def _dot(a, b):
    return jnp.dot(a, b, preferred_element_type=F32)


def _proj_kernel(x_ref, g_ref, w_ref, cos_ref, sa_ref, sb_ref,
                 ka32_ref, kb32_ref, va32_ref, vb32_ref,
                 qa_ref, qb_ref, ka_ref, kb_ref, va_ref, vb_ref):
    x = x_ref[...]
    ms = jnp.mean(x * x, axis=-1, keepdims=True)
    h = (x * lax.rsqrt(ms + EPS) * g_ref[...]).astype(BF16)
    proj = _dot(h, w_ref[...])
    cos = cos_ref[...]
    sa = sa_ref[...]
    sb = sb_ref[...]
    ones = jnp.ones((x.shape[0], LANES), BF16)

    def rot(c):
        return c * cos + pltpu.roll(c, 8, 1) * sa + pltpu.roll(c, LANES - 8, 1) * sb

    for j in range(4):
        lo = j * LANES
        qa = rot(proj[:, lo:lo + LANES]) * (DHA ** -0.5)
        qa_ref[j] = qa.astype(BF16)
        qb = proj[:, 512 + lo:512 + lo + LANES] * (DHB ** -0.5)
        qb_ref[j] = qb.astype(BF16)
        ka = rot(proj[:, 1024 + lo:1024 + lo + LANES])
        ka32_ref[:, lo:lo + LANES] = ka
        ka_ref[j] = ka.astype(BF16)
        kb = proj[:, 1536 + lo:1536 + lo + LANES]
        kb32_ref[:, lo:lo + LANES] = kb
        kb_ref[j] = kb.astype(BF16)
        va = proj[:, 2048 + lo:2048 + lo + LANES]
        va32_ref[:, lo:lo + LANES] = va
        va_ref[j, :, :LANES] = va.astype(BF16)
        va_ref[j, :, LANES:] = ones
        vb = proj[:, 2560 + lo:2560 + lo + LANES]
        vb32_ref[:, lo:lo + LANES] = vb
        vb_ref[j] = vb.astype(BF16)


def _proj(x, g_mix, w_in_bf, tabs, tile, tab_index):
    rows, d = x.shape
    n_tiles = rows // tile
    row_spec = lambda w: pl.BlockSpec((tile, w), lambda i: (i, 0))
    head_spec = lambda w: pl.BlockSpec((4, tile, w), lambda i: (0, i, 0))
    tab_spec = pl.BlockSpec((tile, LANES), lambda i: (tab_index(i), 0))
    const = lambda shape: pl.BlockSpec(shape, lambda i: (0,) * len(shape))
    f32_rows = jax.ShapeDtypeStruct((rows, 512), F32)
    bf_heads = jax.ShapeDtypeStruct((4, rows, LANES), BF16)
    return pl.pallas_call(
        _proj_kernel,
        grid=(n_tiles,),
        in_specs=[row_spec(d), const((1, d)), const(w_in_bf.shape), tab_spec, tab_spec, tab_spec],
        out_specs=[row_spec(512)] * 4 + [head_spec(LANES)] * 4 + [head_spec(2 * LANES), head_spec(LANES)],
        out_shape=[f32_rows] * 4 + [bf_heads] * 4
        + [jax.ShapeDtypeStruct((4, rows, 2 * LANES), BF16), bf_heads],
        compiler_params=_cparams(("parallel",)),
        name="proj",
    )(x, g_mix.reshape(1, d), w_in_bf, *tabs)


def _rotary_tables(pos):
    half = ROT_DIM // 2
    inv_freq = ROPE_THETA ** (-jnp.arange(0, ROT_DIM, 2, dtype=F32) / ROT_DIM)
    ang = pos.astype(F32)[:, None] * inv_freq[None, :]
    cos, sin = jnp.cos(ang), jnp.sin(ang)
    lane = jnp.arange(LANES) % DHA
    idx = lane % half
    first = (lane < half)[None, :]
    second = ((lane >= half) & (lane < ROT_DIM))[None, :]
    cos_t = jnp.where(first | second, cos[:, idx], 1.0)
    sa_t = jnp.where(second, sin[:, idx], 0.0)
    sb_t = jnp.where(first, -sin[:, idx], 0.0)
    return cos_t, sa_t, sb_t


def _stack_masked(q, split):
    lane = lax.broadcasted_iota(jnp.int32, q.shape, 1)
    zero = jnp.zeros_like(q)
    return jnp.concatenate([jnp.where(lane < split, q, zero), jnp.where(lane >= split, q, zero)], axis=0)


def _diff_update(q2, k, vaug, m_ref, acc_ref, mask):
    s = _nt_dot(q2, k)
    if mask is not None:
        s = jnp.where(mask, s, NEG_BIG)
    m_prev = m_ref[...]
    m_new = jnp.maximum(m_prev, jnp.max(s, axis=-1, keepdims=True))
    alpha = jnp.exp(m_prev - m_new)
    p = jnp.exp(s - m_new[:, :1])
    pv = _dot(p.astype(BF16), vaug)
    acc_ref[...] = jnp.concatenate([alpha, alpha], axis=1) * acc_ref[...] + pv
    m_ref[...] = m_new


def _diff_finish(acc_ref, lam_ref, g_ref, t):
    lq1, lk1, lq2, lk2 = (lam_ref[i:i + 1, :] for i in range(4))
    lam = (jnp.exp(jnp.sum(lq1 * lk1, axis=-1, keepdims=True))
           - jnp.exp(jnp.sum(lq2 * lk2, axis=-1, keepdims=True)) + LAM_INIT)
    acc = acc_ref[...]
    o0 = acc[:t, :LANES] / acc[:t, LANES:]
    o1 = acc[t:, :LANES] / acc[t:, LANES:]
    o = o0 - lam * o1
    ms = jnp.mean(o * o, axis=-1, keepdims=True)
    return o * lax.rsqrt(ms + EPS) * g_ref[...] * (1.0 - LAM_INIT)


def _tri_incl(n):
    r = lax.broadcasted_iota(jnp.int32, (n, n), 0)
    c = lax.broadcasted_iota(jnp.int32, (n, n), 1)
    return jnp.where(r >= c, 1.0, 0.0).astype(BF16)


def _sb_update(q2, k, v, tri, carry_ref, acc_ref, mask):
    z = _nt_dot(q2, k)
    nlk = jnp.maximum(z, 0.0) + jnp.log1p(jnp.exp(-jnp.abs(z)))
    if mask is not None:
        nlk = jnp.where(mask, nlk, 0.0)
    hi = nlk.astype(BF16)
    lo = (nlk - hi.astype(F32)).astype(BF16)
    incl = _dot(hi, tri) + _dot(lo, tri)
    carry = carry_ref[...]
    a = jnp.exp(z - incl - carry[:, :1])
    if mask is not None:
        a = jnp.where(mask, a, 0.0)
    acc_ref[...] += _dot(a.astype(BF16), v)
    carry_ref[...] = carry + incl[:, :1]


def _sb_finish(acc_ref, t):
    acc = acc_ref[...]
    lane = lax.broadcasted_iota(jnp.int32, (t, LANES), 1)
    return jnp.where(lane < DHB, acc[:t], acc[t:])


def _diff_prompt_kernel(q_ref, k_ref, v_ref, mk_ref, mv_ref, lam_ref, g_ref, o_ref,
                        m_ref, acc_ref, *, tq, tk, nh):
    i = pl.program_id(2)
    q2 = [_stack_masked(q_ref[h], DHA) for h in range(nh)]
    m_ref[...] = jnp.full(m_ref.shape, NEG_BIG, F32)
    acc_ref[...] = jnp.zeros(acc_ref.shape, F32)
    for h in range(nh):
        _diff_update(q2[h], mk_ref[h], mv_ref[h], m_ref.at[h], acc_ref.at[h], None)

    per_q = tq // tk

    def body(j, c):
        start = pl.multiple_of(j * tk, tk)
        for h in range(nh):
            _diff_update(q2[h], k_ref[h, pl.ds(start, tk), :], v_ref[h, pl.ds(start, tk), :],
                         m_ref.at[h], acc_ref.at[h], None)
        return c

    lax.fori_loop(0, i * per_q, body, 0)

    row = lax.broadcasted_iota(jnp.int32, (tq, tk), 0)
    col = lax.broadcasted_iota(jnp.int32, (tq, tk), 1)
    for d in range(per_q):
        mask = ((col + d * tk) // CHUNK) <= (row // CHUNK)
        mask2 = jnp.concatenate([mask, mask], axis=0)
        start = pl.multiple_of((i * per_q + d) * tk, tk)
        for h in range(nh):
            _diff_update(q2[h], k_ref[h, pl.ds(start, tk), :], v_ref[h, pl.ds(start, tk), :],
                         m_ref.at[h], acc_ref.at[h], mask2)

    for h in range(nh):
        o_ref[:, h * LANES:(h + 1) * LANES] = _diff_finish(acc_ref.at[h], lam_ref, g_ref, tq).astype(o_ref.dtype)


def _diff_prompt(qa, ka, va, mka, mva, lams, g_subln, batch, seq, tq, tk, nh=2):
    nq = seq // tq
    kern = functools.partial(_diff_prompt_kernel, tq=tq, tk=tk, nh=nh)
    return pl.pallas_call(
        kern,
        grid=(batch, HA // nh, nq),
        in_specs=[
            pl.BlockSpec((nh, tq, LANES), lambda b, h, i: (h, b * nq + i, 0)),
            pl.BlockSpec((nh, seq, LANES), lambda b, h, i: (h, b, 0)),
            pl.BlockSpec((nh, seq, 2 * LANES), lambda b, h, i: (h, b, 0)),
            pl.BlockSpec((nh, N_META, LANES), lambda b, h, i: (h, 0, 0)),
            pl.BlockSpec((nh, N_META, 2 * LANES), lambda b, h, i: (h, 0, 0)),
            pl.BlockSpec((4, DHA), lambda b, h, i: (0, 0)),
            pl.BlockSpec((1, DVA), lambda b, h, i: (0, 0)),
        ],
        out_specs=pl.BlockSpec((tq, nh * LANES), lambda b, h, i: (b * nq + i, h)),
        out_shape=jax.ShapeDtypeStruct((batch * seq, HA * DVA), BF16),
        scratch_shapes=[pltpu.VMEM((nh, 2 * tq, LANES), F32), pltpu.VMEM((nh, 2 * tq, 2 * LANES), F32)],
        compiler_params=_cparams(("parallel", "parallel", "arbitrary")),
        name="diff_prompt",
    )(qa, ka, va, mka, mva, lams, g_subln.reshape(1, DVA))


def _sb_live(carry_ref):
    return (jnp.min(carry_ref[...]) < SB_DEAD).astype(jnp.int32)


def _sb_prompt_kernel(q_ref, k_ref, v_ref, mk_ref, mv_ref, o_ref, carry_ref, acc_ref, *, tq):
    i = pl.program_id(2)
    seg = SB_SEG
    per_q = tq // seg
    q2 = _stack_masked(q_ref[0], DHB)
    tri = _tri_incl(seg)
    carry_ref[...] = jnp.zeros(carry_ref.shape, F32)
    acc_ref[...] = jnp.zeros(acc_ref.shape, F32)

    row = lax.broadcasted_iota(jnp.int32, (tq, seg), 0)
    col = lax.broadcasted_iota(jnp.int32, (tq, seg), 1)
    for d in reversed(range(per_q)):
        mask = (col + d * seg) < row
        mask2 = jnp.concatenate([mask, mask], axis=0)
        start = pl.multiple_of((i * per_q + d) * seg, seg)
        _sb_update(q2, k_ref[0, pl.ds(start, seg), :], v_ref[0, pl.ds(start, seg), :], tri,
                   carry_ref, acc_ref, mask2)

    n_full = i * per_q

    def cond(state):
        t, live = state
        return jnp.logical_and(t < n_full, live > 0)

    def body(state):
        t, _ = state
        start = pl.multiple_of((n_full - 1 - t) * seg, seg)
        _sb_update(q2, k_ref[0, pl.ds(start, seg), :], v_ref[0, pl.ds(start, seg), :], tri,
                   carry_ref, acc_ref, None)
        return t + 1, _sb_live(carry_ref)

    lax.while_loop(cond, body, (jnp.int32(0), _sb_live(carry_ref)))
    _sb_update(q2, mk_ref[0], mv_ref[0], tri[:N_META, :N_META], carry_ref, acc_ref, None)
    o_ref[...] = _sb_finish(acc_ref, tq).astype(o_ref.dtype)


def _sb_prompt(qb, kb, vb, mkb, mvb, batch, seq, tq):
    nq = seq // tq
    kern = functools.partial(_sb_prompt_kernel, tq=tq)
    return pl.pallas_call(
        kern,
        grid=(batch, HB // 2, nq),
        in_specs=[
            pl.BlockSpec((1, tq, LANES), lambda b, h, i: (h, b * nq + i, 0)),
            pl.BlockSpec((1, seq, LANES), lambda b, h, i: (h, b, 0)),
            pl.BlockSpec((1, seq, LANES), lambda b, h, i: (h, b, 0)),
            pl.BlockSpec((1, N_META, LANES), lambda b, h, i: (h, 0, 0)),
            pl.BlockSpec((1, N_META, LANES), lambda b, h, i: (h, 0, 0)),
        ],
        out_specs=pl.BlockSpec((tq, LANES), lambda b, h, i: (b * nq + i, h)),
        out_shape=jax.ShapeDtypeStruct((batch * seq, HB * DHB), BF16),
        scratch_shapes=[pltpu.VMEM((2 * tq, LANES), F32), pltpu.VMEM((2 * tq, LANES), F32)],
        compiler_params=_cparams(("parallel", "parallel", "arbitrary")),
        name="sb_prompt",
    )(qb, kb, vb, mkb, mvb)


def _sample_kernel(qa_ref, qb_ref, nka_ref, nva_ref, nkb_ref, nvb_ref,
                   cka_ref, cva_ref, ckb_ref, cvb_ref,
                   mka_ref, mva_ref, mkb_ref, mvb_ref, lam_ref, g_ref,
                   oa_ref, ob_ref, m_ref, acca_ref, carry_ref, accb_ref, *, ns, past):
    seg = SB_SEG
    n_seg = past // seg
    row = lax.broadcasted_iota(jnp.int32, (ns, ns), 0)
    col = lax.broadcasted_iota(jnp.int32, (ns, ns), 1)

    q2 = _stack_masked(qa_ref[0], DHA)
    m_ref[...] = jnp.full(m_ref.shape, NEG_BIG, F32)
    acca_ref[...] = jnp.zeros(acca_ref.shape, F32)
    _diff_update(q2, mka_ref[0], mva_ref[0], m_ref, acca_ref, None)
    ones = jnp.ones((seg, LANES), BF16)
    for s in range(n_seg):
        k = cka_ref[0, s * seg:(s + 1) * seg, :].astype(BF16)
        v = cva_ref[0, s * seg:(s + 1) * seg, :].astype(BF16)
        _diff_update(q2, k, jnp.concatenate([v, ones], axis=1), m_ref, acca_ref, None)
    mask = ((past + col) // CHUNK) <= ((past + row) // CHUNK)
    _diff_update(q2, nka_ref[0], nva_ref[0], m_ref, acca_ref, jnp.concatenate([mask, mask], axis=0))
    oa_ref[...] = _diff_finish(acca_ref, lam_ref, g_ref, ns).astype(oa_ref.dtype)

    q2 = _stack_masked(qb_ref[0], DHB)
    tri = _tri_incl(seg)
    carry_ref[...] = jnp.zeros(carry_ref.shape, F32)
    accb_ref[...] = jnp.zeros(accb_ref.shape, F32)
    strict = col < row
    _sb_update(q2, nkb_ref[0], nvb_ref[0], tri[:ns, :ns], carry_ref, accb_ref,
               jnp.concatenate([strict, strict], axis=0))
    for s in reversed(range(n_seg)):
        @pl.when(_sb_live(carry_ref) > 0)
        def _(s=s):
            k = ckb_ref[0, s * seg:(s + 1) * seg, :].astype(BF16)
            v = cvb_ref[0, s * seg:(s + 1) * seg, :].astype(BF16)
            _sb_update(q2, k, v, tri, carry_ref, accb_ref, None)
    _sb_update(q2, mkb_ref[0], mvb_ref[0], tri[:N_META, :N_META], carry_ref, accb_ref, None)
    ob_ref[...] = _sb_finish(accb_ref, ns).astype(ob_ref.dtype)


def _sample_attention(qa, qb, ka, va, kb, vb, cka, cva, ckb, cvb, mka, mva, mkb, mvb, lams, g_subln,
                      row0, nb, ns, past):
    blk0 = row0 // ns
    new = lambda w: pl.BlockSpec((1, ns, w), lambda b, h: (h, blk0 + b, 0))
    cache = pl.BlockSpec((1, past, LANES), lambda b, h: (b, 0, h))
    meta = lambda w: pl.BlockSpec((1, N_META, w), lambda b, h: (h, 0, 0))
    out = pl.BlockSpec((ns, LANES), lambda b, h: (b, h))
    kern = functools.partial(_sample_kernel, ns=ns, past=past)
    return pl.pallas_call(
        kern,
        grid=(nb, 4),
        in_specs=[new(LANES), new(LANES), new(LANES), new(2 * LANES), new(LANES), new(LANES),
                  cache, cache, cache, cache,
                  meta(LANES), meta(2 * LANES), meta(LANES), meta(LANES),
                  pl.BlockSpec((4, DHA), lambda b, h: (0, 0)),
                  pl.BlockSpec((1, DVA), lambda b, h: (0, 0))],
        out_specs=[out, out],
        out_shape=[jax.ShapeDtypeStruct((nb * ns, 512), BF16)] * 2,
        scratch_shapes=[pltpu.VMEM((2 * ns, LANES), F32), pltpu.VMEM((2 * ns, 2 * LANES), F32),
                        pltpu.VMEM((2 * ns, LANES), F32), pltpu.VMEM((2 * ns, LANES), F32)],
        compiler_params=_cparams(("parallel", "parallel")),
        name="sample_attn",
    )(qa, qb, ka, va, kb, vb, cka, cva, ckb, cvb, mka, mva, mkb, mvb, lams, g_subln.reshape(1, DVA))


def _merge_kernel(x_ref, oa_ref, ob_ref, gmix_ref, wg_ref, bg_ref, wa_ref, wb_ref, wo_ref,
                  gffn_ref, wrt_ref, br_ref, x2_ref, h2_ref, gates_ref):
    x = x_ref[...]
    d = x.shape[1]
    ms = jnp.mean(x * x, axis=-1, keepdims=True)
    h = (x * lax.rsqrt(ms + EPS) * gmix_ref[...]).astype(BF16)
    gate = jax.nn.sigmoid(_dot(h, wg_ref[...]) + bg_ref[...])
    merged = gate[:, :d] * _dot(oa_ref[...], wa_ref[...]) + gate[:, d:] * _dot(ob_ref[...], wb_ref[...])
    x2 = x + _dot(merged.astype(BF16), wo_ref[...])
    x2_ref[...] = x2
    ms2 = jnp.mean(x2 * x2, axis=-1, keepdims=True)
    h2 = x2 * lax.rsqrt(ms2 + EPS) * gffn_ref[...]
    h2_ref[...] = h2.astype(BF16)

    h_hi = h2.astype(BF16)
    h_lo = (h2 - h_hi.astype(F32)).astype(BF16)
    wr = wrt_ref[...]
    w_hi = wr.astype(BF16)
    w_lo = (wr - w_hi.astype(F32)).astype(BF16)
    logits = _nt_dot(w_hi, h_hi) + _nt_dot(w_hi, h_lo) + _nt_dot(w_lo, h_hi) + br_ref[...]

    sub = lax.broadcasted_iota(jnp.int32, logits.shape, 0).astype(F32)
    work = logits
    picks = []
    for _ in range(TOP_K):
        top = jnp.max(work, axis=0, keepdims=True)
        first = jnp.min(jnp.where(work == top, sub, float(N_EXPERTS)), axis=0, keepdims=True)
        hit = sub == first
        picks.append((top, hit))
        work = jnp.where(hit, -jnp.inf, work)
    exps = [jnp.exp(v - picks[0][0]) for v, _ in picks]
    denom = exps[0] + exps[1] + exps[2] + exps[3]
    gates = jnp.zeros(logits.shape, F32)
    for e, (_, hit) in zip(exps, picks):
        gates = gates + jnp.where(hit, e / denom, 0.0)
    gates_ref[...] = gates


def _merge(x, oa, ob, g_mix, w_gate, b_gate, w_a, w_b, w_out, g_ffn, w_router, b_router, tile):
    rows, d = x.shape
    row = lambda w: pl.BlockSpec((tile, w), lambda i: (i, 0))
    const = lambda shape: pl.BlockSpec(shape, lambda i: (0,) * len(shape))
    return pl.pallas_call(
        _merge_kernel,
        grid=(rows // tile,),
        in_specs=[row(d), row(512), row(512), const((1, d)), const((d, 2 * d)), const((1, 2 * d)),
                  const((512, d)), const((512, d)), const((d, d)), const((1, d)),
                  const((N_EXPERTS, d)), const((N_EXPERTS, 1))],
        out_specs=[row(d), row(d), pl.BlockSpec((N_EXPERTS, tile), lambda i: (0, i))],
        out_shape=[jax.ShapeDtypeStruct((rows, d), F32), jax.ShapeDtypeStruct((rows, d), BF16),
                   jax.ShapeDtypeStruct((N_EXPERTS, rows), F32)],
        compiler_params=_cparams(("parallel",)),
        name="merge",
    )(x, oa, ob, g_mix.reshape(1, d), w_gate, b_gate.reshape(1, 2 * d), w_a, w_b, w_out, g_ffn.reshape(1, d),
      w_router.T, b_router.reshape(N_EXPERTS, 1))


def _deinterleave_kernel(w_ref, glu_ref, lin_ref):
    r = lax.broadcasted_iota(jnp.int32, (2 * LANES, 2 * LANES), 0)
    c = lax.broadcasted_iota(jnp.int32, (2 * LANES, 2 * LANES), 1)
    src = jnp.where(c < LANES, 2 * c, 2 * (c - LANES) + 1)
    sel = jnp.where(r == src, 1.0, 0.0).astype(BF16)
    for j in range(w_ref.shape[2] // (2 * LANES)):
        chunk = w_ref[0, :, j * 2 * LANES:(j + 1) * 2 * LANES].astype(BF16)
        both = _dot(chunk, sel)
        glu_ref[0, :, j * LANES:(j + 1) * LANES] = both[:, :LANES].astype(BF16)
        lin_ref[0, :, j * LANES:(j + 1) * LANES] = both[:, LANES:].astype(BF16)


def _deinterleave(w_up):
    ne, d, two_f = w_up.shape
    cols = 512
    out = jax.ShapeDtypeStruct((ne, d, two_f // 2), BF16)
    return pl.pallas_call(
        _deinterleave_kernel,
        grid=(ne, two_f // cols),
        in_specs=[pl.BlockSpec((1, d, cols), lambda e, j: (e, 0, j))],
        out_specs=[pl.BlockSpec((1, d, cols // 2), lambda e, j: (e, 0, j))] * 2,
        out_shape=[out, out],
        compiler_params=_cparams(("parallel", "parallel")),
        name="deinterleave",
    )(w_up)


def _to_bf16_kernel(w_ref, o_ref):
    o_ref[...] = w_ref[...].astype(BF16)


def _to_bf16(w):
    ne, a, b = w.shape
    spec = pl.BlockSpec((1, a, b), lambda e: (e, 0, 0))
    return pl.pallas_call(
        _to_bf16_kernel, grid=(ne,), in_specs=[spec], out_specs=spec,
        out_shape=jax.ShapeDtypeStruct(w.shape, BF16), compiler_params=_cparams(("parallel",)), name="to_bf16",
    )(w)


def _moe_kernel(h_ref, gt_ref, x2_ref, tri_ref, wg_ref, wl_ref, bg_ref, bl_ref, wd_ref, bd_ref, gf_ref,
                y_ref, rank_ref, acc_ref):
    e = pl.program_id(1)
    tb = h_ref.shape[0]
    ch = MOE_CHUNK

    @pl.when(e == 0)
    def _():
        acc_ref[...] = jnp.zeros(acc_ref.shape, F32)
        member = jnp.where(gt_ref[...] > 0.0, 1.0, 0.0).astype(BF16)
        rank_ref[...] = _dot(member, tri_ref[...])

    gate_e = gt_ref[pl.ds(e, 1), :]
    routed = gate_e > 0.0
    key = jnp.where(routed, rank_ref[pl.ds(e, 1), :], -1.0)
    n = jnp.sum(jnp.where(routed, 1.0, 0.0)).astype(jnp.int32)

    def body(c, carry):
        slot = (lax.broadcasted_iota(jnp.int32, (ch, tb), 0) + c * ch).astype(F32)
        sel = key == slot
        g = jnp.where(sel, 1.0, 0.0).astype(BF16)
        xg = _dot(g, h_ref[...]).astype(BF16)
        glu = jnp.minimum(_dot(xg, wg_ref[0]) + bg_ref[0], SWIGLU_LIMIT)
        lin = jnp.clip(_dot(xg, wl_ref[0]) + bl_ref[0], -SWIGLU_LIMIT, SWIGLU_LIMIT)
        act = glu * jax.nn.sigmoid(SWIGLU_ALPHA * glu) * (lin + 1.0)
        y = _dot(act.astype(BF16), wd_ref[0]) + bd_ref[0]
        w = jnp.sum(jnp.where(sel, gate_e, 0.0), axis=1, keepdims=True)
        ys = (y * w).astype(BF16)
        acc_ref[...] += lax.dot_general(g, ys, (((0,), (0,)), ((), ())), preferred_element_type=F32)
        return carry

    lax.fori_loop(0, (n + ch - 1) // ch, body, 0)

    @pl.when(e == pl.num_programs(1) - 1)
    def _():
        x3 = x2_ref[...] + acc_ref[...]
        ms = jnp.mean(x3 * x3, axis=-1, keepdims=True)
        y_ref[...] = x3 * lax.rsqrt(ms + EPS) * gf_ref[...]


def _moe(h2, gates_t, x2, w_glu, w_lin, b_glu, b_lin, w_down, b_down, g_final, tile):
    rows, d = x2.shape
    dff = w_glu.shape[2]
    row = lambda w: pl.BlockSpec((tile, w), lambda i, e: (i, 0))
    exp = lambda a, b: pl.BlockSpec((1, a, b), lambda i, e: (e, 0, 0))
    r = lax.broadcasted_iota(jnp.int32, (tile, tile), 0)
    c = lax.broadcasted_iota(jnp.int32, (tile, tile), 1)
    tri = jnp.where(r < c, 1.0, 0.0).astype(BF16)
    return pl.pallas_call(
        _moe_kernel,
        grid=(rows // tile, N_EXPERTS),
        in_specs=[row(d), pl.BlockSpec((N_EXPERTS, tile), lambda i, e: (0, i)), row(d),
                  pl.BlockSpec((tile, tile), lambda i, e: (0, 0)),
                  exp(d, dff), exp(d, dff), exp(1, dff), exp(1, dff),
                  exp(dff, d), exp(1, d), pl.BlockSpec((1, d), lambda i, e: (0, 0))],
        out_specs=row(d),
        out_shape=jax.ShapeDtypeStruct((rows, d), F32),
        scratch_shapes=[pltpu.VMEM((N_EXPERTS, tile), F32), pltpu.VMEM((tile, d), F32)],
        compiler_params=_cparams(("parallel", "arbitrary")),
        name="moe",
    )(h2, gates_t, x2, tri, w_glu, w_lin, b_glu, b_lin, w_down, b_down, g_final.reshape(1, d))


def _pick(n, pref):
    t = min(n, pref)
    assert n % t == 0, (n, pref)
    return t


def kernel(x_prompt, x_sample, cache_diff_k, cache_diff_v, cache_sb_k, cache_sb_v, meta_tokens, g_mix, w_in,
           lambda_q1, lambda_k1, lambda_q2, lambda_k2, g_subln, w_branch_a, w_branch_b, w_gate, b_gate, w_out,
           g_ffn, w_router, b_router, w_up, b_up, w_down, b_down, g_final):
    b, n, d = x_prompt.shape
    bs, ns, _ = x_sample.shape
    past = cache_diff_k.shape[1]
    rows_p, rows_s = b * n, bs * ns
    assert n % SB_SEG == 0 and past % SB_SEG == 0

    w_in_bf = w_in.astype(BF16)
    lams = jnp.stack([lambda_q1, lambda_k1, lambda_q2, lambda_k2]).astype(F32)

    tile_p = _pick(n, 512)
    per_seq = n // tile_p
    p32 = _proj(x_prompt.reshape(rows_p, d), g_mix, w_in_bf, _rotary_tables(N_META + jnp.arange(n)),
                tile_p, lambda i: i % per_seq)
    tile_s = _pick(rows_s, 512)
    assert tile_s % ns == 0
    s32 = _proj(x_sample.reshape(rows_s, d), g_mix, w_in_bf,
                _rotary_tables(N_META + past + (jnp.arange(tile_s) % ns)), tile_s, lambda i: 0)
    m32 = _proj(meta_tokens, g_mix, w_in_bf, _rotary_tables(jnp.arange(N_META)), N_META, lambda i: i)
    _, _, _, _, qa, qb, ka, kb, va, vb = p32
    _, _, _, _, sqa, sqb, ska, skb, sva, svb = s32
    _, _, _, _, _, _, mka, mkb, mva, mvb = m32

    tq = _pick(n, 512)
    oa_p = _diff_prompt(qa, ka, va, mka, mva, lams, g_subln, b, n, tq, _pick(tq, 512))
    ob_p = _sb_prompt(qb, kb, vb, mkb, mvb, b, n, tq)
    oa_s, ob_s = _sample_attention(
        sqa, sqb, ska, sva, skb, svb,
        cache_diff_k.reshape(bs, past, 512), cache_diff_v.reshape(bs, past, 512),
        cache_sb_k.reshape(bs, past, 512), cache_sb_v.reshape(bs, past, 512),
        mka, mva, mkb, mvb, lams, g_subln, 0, bs, ns, past)

    dff = w_down.shape[1]
    w_glu, w_lin = _deinterleave(w_up)
    w_down_bf = _to_bf16(w_down)
    b_glu = b_up[:, 0::2].reshape(N_EXPERTS, 1, dff)
    b_lin = b_up[:, 1::2].reshape(N_EXPERTS, 1, dff)
    merge_w = (g_mix, w_gate.astype(BF16), b_gate, w_branch_a.astype(BF16), w_branch_b.astype(BF16),
               w_out.astype(BF16), g_ffn, w_router, b_router)

    def ffn(x_rows, oa, ob, tile):
        x2, h2, gates_t = _merge(x_rows, oa, ob, *merge_w, tile)
        return _moe(h2, gates_t, x2, w_glu, w_lin, b_glu, b_lin, w_down_bf, b_down.reshape(N_EXPERTS, 1, d),
                    g_final, _pick(x_rows.shape[0], 1024))

    y_p = ffn(x_prompt.reshape(rows_p, d), oa_p, ob_p, tile_p)
    y_s = ffn(x_sample.reshape(rows_s, d), oa_s, ob_s, tile_s)

    def with_meta(meta_rows, rows32, shape):
        full = jnp.concatenate([jnp.broadcast_to(meta_rows[None], (b, N_META, 512)), rows32.reshape(b, n, 512)], axis=1)
        return full.reshape((b, N_META + n) + shape)

    shapes = ((HA, 2, DHA), (HB, DHB), (HA, DVA), (HB, DHB))
    kv_p = [with_meta(m32[j], p32[j], shapes[j]) for j in range(4)]
    kv_s = [s32[j].reshape((bs, ns) + shapes[j]) for j in range(4)]
    return (y_p.reshape(b, n, d), y_s.reshape(bs, ns, d), kv_p[0], kv_p[2], kv_p[1], kv_p[3],
            kv_s[0], kv_s[2], kv_s[1], kv_s[3])
```

```python
import functools
import math

import jax
import jax.numpy as jnp
from jax import lax
from jax.experimental import pallas as pl
from jax.experimental.pallas import tpu as pltpu

F32 = jnp.float32
BF16 = jnp.bfloat16

N_META = 16
CHUNK = 64
HA = 4
DHA = 64
DVA = 128
HB = 8
DHB = 64
ROT_DIM = 16
ROPE_THETA = 500000.0
N_EXPERTS = 32
TOP_K = 4
SWIGLU_ALPHA = 1.702
SWIGLU_LIMIT = 7.0
EPS = 1e-5
LAM_INIT = 0.8 - 0.6 * math.exp(-0.3 * 0)
LOG2E = 1.4426950408889634

LANES = 128
NEG_BIG = -1e30
SB_SEG = 256
SB_DEAD = 150.0
VMEM_LIMIT = 56 * 1024 * 1024
MOE_CHUNK = 192


def _cparams(sem):
    return pltpu.CompilerParams(dimension_semantics=sem, vmem_limit_bytes=VMEM_LIMIT)


def _nt_dot(a, b):
    return lax.dot_general(a, b, (((1,), (1,)), ((), ())), preferred_element_type=F32)


def _dot(a, b):
    return jnp.dot(a, b, preferred_element_type=F32)


def _proj_kernel(x_ref, g_ref, w_ref, cos_ref, sa_ref, sb_ref,
                 ka32_ref, kb32_ref, va32_ref, vb32_ref,
                 qa_ref, qb_ref, ka_ref, kb_ref, va_ref, vb_ref):
    x = x_ref[...]
    ms = jnp.mean(x * x, axis=-1, keepdims=True)
    h = (x * lax.rsqrt(ms + EPS) * g_ref[...]).astype(BF16)
    proj = _dot(h, w_ref[...])
    cos = cos_ref[...]
    sa = sa_ref[...]
    sb = sb_ref[...]
    ones = jnp.ones((x.shape[0], LANES), BF16)

    def rot(c):
        return c * cos + pltpu.roll(c, 8, 1) * sa + pltpu.roll(c, LANES - 8, 1) * sb

    for j in range(4):
        lo = j * LANES
        qa = rot(proj[:, lo:lo + LANES]) * (DHA ** -0.5 * LOG2E)
        qa_ref[j] = qa.astype(BF16)
        qb = proj[:, 512 + lo:512 + lo + LANES] * (DHB ** -0.5 * LOG2E)
        qb_ref[j] = qb.astype(BF16)
        ka = rot(proj[:, 1024 + lo:1024 + lo + LANES])
        ka32_ref[:, lo:lo + LANES] = ka
        ka_ref[j] = ka.astype(BF16)
        kb = proj[:, 1536 + lo:1536 + lo + LANES]
        kb32_ref[:, lo:lo + LANES] = kb
        kb_ref[j] = kb.astype(BF16)
        va = proj[:, 2048 + lo:2048 + lo + LANES]
        va32_ref[:, lo:lo + LANES] = va
        va_ref[j, :, :LANES] = va.astype(BF16)
        va_ref[j, :, LANES:] = ones
        vb = proj[:, 2560 + lo:2560 + lo + LANES]
        vb32_ref[:, lo:lo + LANES] = vb
        vb_ref[j] = vb.astype(BF16)


def _proj(x, g_mix, w_in_bf, tabs, tile, tab_index):
    rows, d = x.shape
    n_tiles = rows // tile
    row_spec = lambda w: pl.BlockSpec((tile, w), lambda i: (i, 0))
    head_spec = lambda w: pl.BlockSpec((4, tile, w), lambda i: (0, i, 0))
    tab_spec = pl.BlockSpec((tile, LANES), lambda i: (tab_index(i), 0))
    const = lambda shape: pl.BlockSpec(shape, lambda i: (0,) * len(shape))
    f32_rows = jax.ShapeDtypeStruct((rows, 512), F32)
    bf_heads = jax.ShapeDtypeStruct((4, rows, LANES), BF16)
    return pl.pallas_call(
        _proj_kernel,
        grid=(n_tiles,),
        in_specs=[row_spec(d), const((1, d)), const(w_in_bf.shape), tab_spec, tab_spec, tab_spec],
        out_specs=[row_spec(512)] * 4 + [head_spec(LANES)] * 4 + [head_spec(2 * LANES), head_spec(LANES)],
        out_shape=[f32_rows] * 4 + [bf_heads] * 4
        + [jax.ShapeDtypeStruct((4, rows, 2 * LANES), BF16), bf_heads],
        compiler_params=_cparams(("parallel",)),
        name="proj",
    )(x, g_mix.reshape(1, d), w_in_bf, *tabs)


def _rotary_tables(pos):
    half = ROT_DIM // 2
    inv_freq = ROPE_THETA ** (-jnp.arange(0, ROT_DIM, 2, dtype=F32) / ROT_DIM)
    ang = pos.astype(F32)[:, None] * inv_freq[None, :]
    cos, sin = jnp.cos(ang), jnp.sin(ang)
    lane = jnp.arange(LANES) % DHA
    idx = lane % half
    first = (lane < half)[None, :]
    second = ((lane >= half) & (lane < ROT_DIM))[None, :]
    cos_t = jnp.where(first | second, cos[:, idx], 1.0)
    sa_t = jnp.where(second, sin[:, idx], 0.0)
    sb_t = jnp.where(first, -sin[:, idx], 0.0)
    return cos_t, sa_t, sb_t


def _stack_masked(q, split):
    lane = lax.broadcasted_iota(jnp.int32, q.shape, 1)
    zero = jnp.zeros_like(q)
    return jnp.concatenate([jnp.where(lane < split, q, zero), jnp.where(lane >= split, q, zero)], axis=0)


def _diff_update(q2, k, vaug, m_ref, acc_ref, mask, k_is_t=False):
    s = _dot(q2, k) if k_is_t else _nt_dot(q2, k)
    if mask is not None:
        s = jnp.where(mask, s, NEG_BIG)
    m_prev = m_ref[...]
    m_new = jnp.maximum(m_prev, jnp.max(s, axis=-1, keepdims=True))
    alpha = jnp.exp2(m_prev - m_new)
    p = jnp.exp2(s - m_new[:, :1])
    pv = _dot(p.astype(BF16), vaug)
    acc_ref[...] = jnp.concatenate([alpha, alpha], axis=1) * acc_ref[...] + pv
    m_ref[...] = m_new


def _diff_finish(acc_ref, lam_ref, g_ref, t):
    lq1, lk1, lq2, lk2 = (lam_ref[i:i + 1, :] for i in range(4))
    lam = (jnp.exp(jnp.sum(lq1 * lk1, axis=-1, keepdims=True))
           - jnp.exp(jnp.sum(lq2 * lk2, axis=-1, keepdims=True)) + LAM_INIT)
    acc = acc_ref[...]
    o0 = acc[:t, :LANES] / acc[:t, LANES:]
    o1 = acc[t:, :LANES] / acc[t:, LANES:]
    o = o0 - lam * o1
    ms = jnp.mean(o * o, axis=-1, keepdims=True)
    return o * lax.rsqrt(ms + EPS) * g_ref[...] * (1.0 - LAM_INIT)


def _tri_incl(n):
    r = lax.broadcasted_iota(jnp.int32, (n, n), 0)
    c = lax.broadcasted_iota(jnp.int32, (n, n), 1)
    return jnp.where(r >= c, 1.0, 0.0).astype(BF16)


def _sb_update(q2, k, v, tri, carry_ref, acc_ref, mask, kv_is_t=False):
    z = _dot(q2, k) if kv_is_t else _nt_dot(q2, k)
    nlk = jnp.maximum(z, 0.0) + jnp.log2(1.0 + jnp.exp2(-jnp.abs(z)))
    if mask is not None:
        nlk = jnp.where(mask, nlk, 0.0)
    hi = nlk.astype(BF16)
    lo = (nlk - hi.astype(F32)).astype(BF16)
    seg = tri.shape[0]
    c = carry_ref[...][:, :1]
    args = []
    for j in reversed(range(z.shape[1] // seg)):
        sl = slice(j * seg, (j + 1) * seg)
        incl = _dot(hi[:, sl], tri) + _dot(lo[:, sl], tri)
        args.append(z[:, sl] - incl - c)
        c = c + incl[:, :1]
    arg = args[0] if len(args) == 1 else jnp.concatenate(args[::-1], axis=1)
    a = jnp.exp2(arg)
    if mask is not None:
        a = jnp.where(mask, a, 0.0)
    a = a.astype(BF16)
    acc_ref[...] += _nt_dot(a, v) if kv_is_t else _dot(a, v)
    carry_ref[...] = jnp.broadcast_to(c, carry_ref.shape)


def _sb_finish(acc_ref, t):
    acc = acc_ref[...]
    lane = lax.broadcasted_iota(jnp.int32, (t, LANES), 1)
    return jnp.where(lane < DHB, acc[:t], acc[t:])


def _diff_prompt_kernel(q_ref, k_ref, v_ref, mk_ref, mv_ref, lam_ref, g_ref, o_ref,
                        m_ref, acc_ref, *, tq, tk, nh):
    i = pl.program_id(2)
    q2 = [_stack_masked(q_ref[h], DHA) for h in range(nh)]
    m_ref[...] = jnp.full(m_ref.shape, NEG_BIG, F32)
    acc_ref[...] = jnp.zeros(acc_ref.shape, F32)
    for h in range(nh):
        _diff_update(q2[h], mk_ref[h], mv_ref[h], m_ref.at[h], acc_ref.at[h], None)

    per_q = tq // tk

    def body(j, c):
        start = pl.multiple_of(j * tk, tk)
        for h in range(nh):
            _diff_update(q2[h], k_ref[h, pl.ds(start, tk), :], v_ref[h, pl.ds(start, tk), :],
                         m_ref.at[h], acc_ref.at[h], None)
        return c

    lax.fori_loop(0, i * per_q, body, 0)

    row = lax.broadcasted_iota(jnp.int32, (tq, tk), 0)
    col = lax.broadcasted_iota(jnp.int32, (tq, tk), 1)
    for d in range(per_q):
        mask = ((col + d * tk) // CHUNK) <= (row // CHUNK)
        mask2 = jnp.concatenate([mask, mask], axis=0)
        start = pl.multiple_of((i * per_q + d) * tk, tk)
        for h in range(nh):
            _diff_update(q2[h], k_ref[h, pl.ds(start, tk), :], v_ref[h, pl.ds(start, tk), :],
                         m_ref.at[h], acc_ref.at[h], mask2)

    for h in range(nh):
        o_ref[:, h * LANES:(h + 1) * LANES] = _diff_finish(acc_ref.at[h], lam_ref, g_ref, tq).astype(o_ref.dtype)


def _diff_prompt(qa, ka, va, mka, mva, lams, g_subln, batch, seq, tq, tk, nh=4):
    nq = seq // tq
    kern = functools.partial(_diff_prompt_kernel, tq=tq, tk=tk, nh=nh)
    return pl.pallas_call(
        kern,
        grid=(batch, HA // nh, nq),
        in_specs=[
            pl.BlockSpec((nh, tq, LANES), lambda b, h, i: (h, b * nq + i, 0)),
            pl.BlockSpec((nh, seq, LANES), lambda b, h, i: (h, b, 0), pipeline_mode=pl.Buffered(1)),
            pl.BlockSpec((nh, seq, 2 * LANES), lambda b, h, i: (h, b, 0), pipeline_mode=pl.Buffered(1)),
            pl.BlockSpec((nh, N_META, LANES), lambda b, h, i: (h, 0, 0)),
            pl.BlockSpec((nh, N_META, 2 * LANES), lambda b, h, i: (h, 0, 0)),
            pl.BlockSpec((4, DHA), lambda b, h, i: (0, 0)),
            pl.BlockSpec((1, DVA), lambda b, h, i: (0, 0)),
        ],
        out_specs=pl.BlockSpec((tq, nh * LANES), lambda b, h, i: (b * nq + i, h)),
        out_shape=jax.ShapeDtypeStruct((batch * seq, HA * DVA), BF16),
        scratch_shapes=[pltpu.VMEM((nh, 2 * tq, LANES), F32), pltpu.VMEM((nh, 2 * tq, 2 * LANES), F32)],
        compiler_params=_cparams(("parallel", "parallel", "arbitrary")),
        name="diff_prompt",
    )(qa, ka, va, mka, mva, lams, g_subln.reshape(1, DVA))


def _sb_live(carry_ref):
    return (jnp.min(carry_ref[...]) < SB_DEAD).astype(jnp.int32)


def _sb_prompt_kernel(q_ref, k_ref, v_ref, mk_ref, mv_ref, o_ref, carry_ref, acc_ref, *, tq):
    i = pl.program_id(2)
    seg = SB_SEG
    per_q = tq // seg
    q2 = _stack_masked(q_ref[0], DHB)
    tri = _tri_incl(seg)
    carry_ref[...] = jnp.zeros(carry_ref.shape, F32)
    acc_ref[...] = jnp.zeros(acc_ref.shape, F32)

    row = lax.broadcasted_iota(jnp.int32, (tq, tq), 0)
    col = lax.broadcasted_iota(jnp.int32, (tq, tq), 1)
    strict = col < row
    start = pl.multiple_of(i * tq, tq)
    _sb_update(q2, k_ref[0, pl.ds(start, tq), :], v_ref[0, pl.ds(start, tq), :], tri,
               carry_ref, acc_ref, jnp.concatenate([strict, strict], axis=0))

    n_full = i * per_q

    def cond(state):
        t, live = state
        return jnp.logical_and(t < n_full, live > 0)

    def body(state):
        t, _ = state
        start = pl.multiple_of((n_full - 1 - t) * seg, seg)
        _sb_update(q2, k_ref[0, pl.ds(start, seg), :], v_ref[0, pl.ds(start, seg), :], tri,
                   carry_ref, acc_ref, None)
        return t + 1, _sb_live(carry_ref)

    lax.while_loop(cond, body, (jnp.int32(0), _sb_live(carry_ref)))

    @pl.when(_sb_live(carry_ref) > 0)
    def _():
        _sb_update(q2, mk_ref[0], mv_ref[0], tri[:N_META, :N_META], carry_ref, acc_ref, None)

    o_ref[...] = _sb_finish(acc_ref, tq).astype(o_ref.dtype)


def _sb_prompt(qb, kb, vb, mkb, mvb, batch, seq, tq):
    nq = seq // tq
    kern = functools.partial(_sb_prompt_kernel, tq=tq)
    return pl.pallas_call(
        kern,
        grid=(batch, HB // 2, nq),
        in_specs=[
            pl.BlockSpec((1, tq, LANES), lambda b, h, i: (h, b * nq + i, 0)),
            pl.BlockSpec((1, seq, LANES), lambda b, h, i: (h, b, 0)),
            pl.BlockSpec((1, seq, LANES), lambda b, h, i: (h, b, 0)),
            pl.BlockSpec((1, N_META, LANES), lambda b, h, i: (h, 0, 0)),
            pl.BlockSpec((1, N_META, LANES), lambda b, h, i: (h, 0, 0)),
        ],
        out_specs=pl.BlockSpec((tq, LANES), lambda b, h, i: (b * nq + i, h)),
        out_shape=jax.ShapeDtypeStruct((batch * seq, HB * DHB), BF16),
        scratch_shapes=[pltpu.VMEM((2 * tq, LANES), F32), pltpu.VMEM((2 * tq, LANES), F32)],
        compiler_params=_cparams(("parallel", "parallel", "arbitrary")),
        name="sb_prompt",
    )(qb, kb, vb, mkb, mvb)


def _sample_kernel(qa_ref, qb_ref, nka_ref, nva_ref, nkb_ref, nvb_ref,
                   cka_ref, cva_ref, ckb_ref, cvb_ref,
                   mka_ref, mva_ref, mkb_ref, mvb_ref, lam_ref, g_ref,
                   oa_ref, ob_ref, m_ref, acca_ref, carry_ref, accb_ref, *, ns, past):
    seg = SB_SEG
    n_seg = past // seg
    row = lax.broadcasted_iota(jnp.int32, (ns, ns), 0)
    col = lax.broadcasted_iota(jnp.int32, (ns, ns), 1)

    q2 = _stack_masked(qa_ref[0], DHA)
    m_ref[...] = jnp.full(m_ref.shape, NEG_BIG, F32)
    acca_ref[...] = jnp.zeros(acca_ref.shape, F32)
    _diff_update(q2, mka_ref[0], mva_ref[0], m_ref, acca_ref, None)
    ones = jnp.ones((seg, LANES), BF16)
    for s in range(n_seg):
        k = cka_ref[0, :, s * seg:(s + 1) * seg].astype(BF16)
        v = cva_ref[0, s * seg:(s + 1) * seg, :].astype(BF16)
        _diff_update(q2, k, jnp.concatenate([v, ones], axis=1), m_ref, acca_ref, None, k_is_t=True)
    mask = ((past + col) // CHUNK) <= ((past + row) // CHUNK)
    _diff_update(q2, nka_ref[0], nva_ref[0], m_ref, acca_ref, jnp.concatenate([mask, mask], axis=0))
    oa_ref[...] = _diff_finish(acca_ref, lam_ref, g_ref, ns).astype(oa_ref.dtype)

    q2 = _stack_masked(qb_ref[0], DHB)
    tri = _tri_incl(seg)
    carry_ref[...] = jnp.zeros(carry_ref.shape, F32)
    accb_ref[...] = jnp.zeros(accb_ref.shape, F32)
    strict = col < row
    _sb_update(q2, nkb_ref[0], nvb_ref[0], tri[:ns, :ns], carry_ref, accb_ref,
               jnp.concatenate([strict, strict], axis=0))
    for s in reversed(range(n_seg)):
        @pl.when(_sb_live(carry_ref) > 0)
        def _(s=s):
            k = ckb_ref[0, :, s * seg:(s + 1) * seg].astype(BF16)
            v = cvb_ref[0, :, s * seg:(s + 1) * seg].astype(BF16)
            _sb_update(q2, k, v, tri, carry_ref, accb_ref, None, kv_is_t=True)

    @pl.when(_sb_live(carry_ref) > 0)
    def _():
        _sb_update(q2, mkb_ref[0], mvb_ref[0], tri[:N_META, :N_META], carry_ref, accb_ref, None)

    ob_ref[...] = _sb_finish(accb_ref, ns).astype(ob_ref.dtype)


def _sample_attention(qa, qb, ka, va, kb, vb, cka, cva, ckb, cvb, mka, mva, mkb, mvb, lams, g_subln,
                      row0, nb, ns, past):
    blk0 = row0 // ns
    new = lambda w: pl.BlockSpec((1, ns, w), lambda b, h: (h, blk0 + b, 0))
    cache = pl.BlockSpec((1, past, LANES), lambda b, h: (b, 0, h))
    cache_t = pl.BlockSpec((1, LANES, past), lambda b, h: (b, h, 0))
    meta = lambda w: pl.BlockSpec((1, N_META, w), lambda b, h: (h, 0, 0))
    out = pl.BlockSpec((ns, LANES), lambda b, h: (b, h))
    kern = functools.partial(_sample_kernel, ns=ns, past=past)
    return pl.pallas_call(
        kern,
        grid=(nb, 4),
        in_specs=[new(LANES), new(LANES), new(LANES), new(2 * LANES), new(LANES), new(LANES),
                  cache_t, cache, cache_t, cache_t,
                  meta(LANES), meta(2 * LANES), meta(LANES), meta(LANES),
                  pl.BlockSpec((4, DHA), lambda b, h: (0, 0)),
                  pl.BlockSpec((1, DVA), lambda b, h: (0, 0))],
        out_specs=[out, out],
        out_shape=[jax.ShapeDtypeStruct((nb * ns, 512), BF16)] * 2,
        scratch_shapes=[pltpu.VMEM((2 * ns, LANES), F32), pltpu.VMEM((2 * ns, 2 * LANES), F32),
                        pltpu.VMEM((2 * ns, LANES), F32), pltpu.VMEM((2 * ns, LANES), F32)],
        compiler_params=_cparams(("parallel", "parallel")),
        name="sample_attn",
    )(qa, qb, ka, va, kb, vb, cka, cva, ckb, cvb, mka, mva, mkb, mvb, lams, g_subln.reshape(1, DVA))


def _merge_kernel(x_ref, oa_ref, ob_ref, gmix_ref, wg_ref, bg_ref, wa_ref, wb_ref, wo_ref,
                  gffn_ref, wrt_ref, br_ref, x2_ref, h2_ref, gates_ref):
    x = x_ref[...]
    d = x.shape[1]
    ms = jnp.mean(x * x, axis=-1, keepdims=True)
    h = (x * lax.rsqrt(ms + EPS) * gmix_ref[...]).astype(BF16)
    gate = jax.nn.sigmoid(_dot(h, wg_ref[...]) + bg_ref[...])
    merged = gate[:, :d] * _dot(oa_ref[...], wa_ref[...]) + gate[:, d:] * _dot(ob_ref[...], wb_ref[...])
    x2 = x + _dot(merged.astype(BF16), wo_ref[...])
    x2_ref[...] = x2
    ms2 = jnp.mean(x2 * x2, axis=-1, keepdims=True)
    h2 = x2 * lax.rsqrt(ms2 + EPS) * gffn_ref[...]
    h2_ref[...] = h2.astype(BF16)

    h_hi = h2.astype(BF16)
    h_lo = (h2 - h_hi.astype(F32)).astype(BF16)
    wr = wrt_ref[...]
    w_hi = wr.astype(BF16)
    w_lo = (wr - w_hi.astype(F32)).astype(BF16)
    logits = _nt_dot(w_hi, h_hi) + _nt_dot(w_hi, h_lo) + _nt_dot(w_lo, h_hi) + br_ref[...]

    sub = lax.broadcasted_iota(jnp.int32, logits.shape, 0).astype(F32)
    work = logits
    picks = []
    for _ in range(TOP_K):
        top = jnp.max(work, axis=0, keepdims=True)
        first = jnp.min(jnp.where(work == top, sub, float(N_EXPERTS)), axis=0, keepdims=True)
        hit = sub == first
        picks.append((top, hit))
        work = jnp.where(hit, -jnp.inf, work)
    exps = [jnp.exp(v - picks[0][0]) for v, _ in picks]
    denom = exps[0] + exps[1] + exps[2] + exps[3]
    gates = jnp.zeros(logits.shape, F32)
    for e, (_, hit) in zip(exps, picks):
        gates = gates + jnp.where(hit, e / denom, 0.0)
    gates_ref[...] = gates


def _merge(x, oa, ob, g_mix, w_gate, b_gate, w_a, w_b, w_out, g_ffn, w_router, b_router, tile):
    rows, d = x.shape
    row = lambda w: pl.BlockSpec((tile, w), lambda i: (i, 0))
    const = lambda shape: pl.BlockSpec(shape, lambda i: (0,) * len(shape))
    return pl.pallas_call(
        _merge_kernel,
        grid=(rows // tile,),
        in_specs=[row(d), row(512), row(512), const((1, d)), const((d, 2 * d)), const((1, 2 * d)),
                  const((512, d)), const((512, d)), const((d, d)), const((1, d)),
                  const((N_EXPERTS, d)), const((N_EXPERTS, 1))],
        out_specs=[row(d), row(d), pl.BlockSpec((N_EXPERTS, tile), lambda i: (0, i))],
        out_shape=[jax.ShapeDtypeStruct((rows, d), F32), jax.ShapeDtypeStruct((rows, d), BF16),
                   jax.ShapeDtypeStruct((N_EXPERTS, rows), F32)],
        compiler_params=_cparams(("parallel",)),
        name="merge",
    )(x, oa, ob, g_mix.reshape(1, d), w_gate, b_gate.reshape(1, 2 * d), w_a, w_b, w_out, g_ffn.reshape(1, d),
      w_router.T, b_router.reshape(N_EXPERTS, 1))


def _deinterleave_kernel(w_ref, glu_ref, lin_ref):
    r = lax.broadcasted_iota(jnp.int32, (2 * LANES, 2 * LANES), 0)
    c = lax.broadcasted_iota(jnp.int32, (2 * LANES, 2 * LANES), 1)
    src = jnp.where(c < LANES, 2 * c, 2 * (c - LANES) + 1)
    sel = jnp.where(r == src, 1.0, 0.0).astype(BF16)
    for j in range(w_ref.shape[2] // (2 * LANES)):
        chunk = w_ref[0, :, j * 2 * LANES:(j + 1) * 2 * LANES].astype(BF16)
        both = _dot(chunk, sel)
        glu_ref[0, :, j * LANES:(j + 1) * LANES] = both[:, :LANES].astype(BF16)
        lin_ref[0, :, j * LANES:(j + 1) * LANES] = both[:, LANES:].astype(BF16)


def _deinterleave(w_up):
    ne, d, two_f = w_up.shape
    cols = two_f
    out = jax.ShapeDtypeStruct((ne, d, two_f // 2), BF16)
    return pl.pallas_call(
        _deinterleave_kernel,
        grid=(ne, two_f // cols),
        in_specs=[pl.BlockSpec((1, d, cols), lambda e, j: (e, 0, j))],
        out_specs=[pl.BlockSpec((1, d, cols // 2), lambda e, j: (e, 0, j))] * 2,
        out_shape=[out, out],
        compiler_params=_cparams(("parallel", "parallel")),
        name="deinterleave",
    )(w_up)


def _to_bf16_kernel(w_ref, o_ref):
    o_ref[...] = w_ref[...].astype(BF16)


def _to_bf16(w):
    ne, a, b = w.shape
    spec = pl.BlockSpec((1, a, b), lambda e: (e, 0, 0))
    return pl.pallas_call(
        _to_bf16_kernel, grid=(ne,), in_specs=[spec], out_specs=spec,
        out_shape=jax.ShapeDtypeStruct(w.shape, BF16), compiler_params=_cparams(("parallel",)), name="to_bf16",
    )(w)


def _moe_kernel(h_ref, gt_ref, x2_ref, tri_ref, wg_ref, wl_ref, bg_ref, bl_ref, wd_ref, bd_ref, gf_ref,
                y_ref, rank_ref, acc_ref):
    e = pl.program_id(1)
    tb = h_ref.shape[0]
    ch = MOE_CHUNK

    @pl.when(e == 0)
    def _():
        acc_ref[...] = jnp.zeros(acc_ref.shape, F32)
        member = jnp.where(gt_ref[...] > 0.0, 1.0, 0.0).astype(BF16)
        rank_ref[...] = _dot(member, tri_ref[...])

    gate_e = gt_ref[pl.ds(e, 1), :]
    routed = gate_e > 0.0
    key = jnp.where(routed, rank_ref[pl.ds(e, 1), :], -1.0)
    n = jnp.sum(jnp.where(routed, 1.0, 0.0)).astype(jnp.int32)

    def body(c, carry):
        slot = (lax.broadcasted_iota(jnp.int32, (ch, tb), 0) + c * ch).astype(F32)
        sel = key == slot
        g = jnp.where(sel, 1.0, 0.0).astype(BF16)
        xg = _dot(g, h_ref[...]).astype(BF16)
        glu = jnp.minimum(_dot(xg, wg_ref[0]) + bg_ref[0], SWIGLU_LIMIT)
        lin = jnp.clip(_dot(xg, wl_ref[0]) + bl_ref[0], -SWIGLU_LIMIT, SWIGLU_LIMIT)
        act = glu * jax.nn.sigmoid(SWIGLU_ALPHA * glu) * (lin + 1.0)
        y = _dot(act.astype(BF16), wd_ref[0]) + bd_ref[0]
        w = jnp.sum(jnp.where(sel, gate_e, 0.0), axis=1, keepdims=True)
        ys = (y * w).astype(BF16)
        acc_ref[...] += lax.dot_general(g, ys, (((0,), (0,)), ((), ())), preferred_element_type=F32)
        return carry

    lax.fori_loop(0, (n + ch - 1) // ch, body, 0)

    @pl.when(e == pl.num_programs(1) - 1)
    def _():
        x3 = x2_ref[...] + acc_ref[...]
        ms = jnp.mean(x3 * x3, axis=-1, keepdims=True)
        y_ref[...] = x3 * lax.rsqrt(ms + EPS) * gf_ref[...]


def _moe(h2, gates_t, x2, w_glu, w_lin, b_glu, b_lin, w_down, b_down, g_final, tile):
    rows, d = x2.shape
    dff = w_glu.shape[2]
    row = lambda w: pl.BlockSpec((tile, w), lambda i, e: (i, 0))
    exp = lambda a, b: pl.BlockSpec((1, a, b), lambda i, e: (e, 0, 0))
    r = lax.broadcasted_iota(jnp.int32, (tile, tile), 0)
    c = lax.broadcasted_iota(jnp.int32, (tile, tile), 1)
    tri = jnp.where(r < c, 1.0, 0.0).astype(BF16)
    return pl.pallas_call(
        _moe_kernel,
        grid=(rows // tile, N_EXPERTS),
        in_specs=[row(d), pl.BlockSpec((N_EXPERTS, tile), lambda i, e: (0, i)), row(d),
                  pl.BlockSpec((tile, tile), lambda i, e: (0, 0)),
                  exp(d, dff), exp(d, dff), exp(1, dff), exp(1, dff),
                  exp(dff, d), exp(1, d), pl.BlockSpec((1, d), lambda i, e: (0, 0))],
        out_specs=row(d),
        out_shape=jax.ShapeDtypeStruct((rows, d), F32),
        scratch_shapes=[pltpu.VMEM((N_EXPERTS, tile), F32), pltpu.VMEM((tile, d), F32)],
        compiler_params=_cparams(("parallel", "arbitrary")),
        name="moe",
    )(h2, gates_t, x2, tri, w_glu, w_lin, b_glu, b_lin, w_down, b_down, g_final.reshape(1, d))


def _pick(n, pref):
    t = min(n, pref)
    assert n % t == 0, (n, pref)
    return t


def kernel(x_prompt, x_sample, cache_diff_k, cache_diff_v, cache_sb_k, cache_sb_v, meta_tokens, g_mix, w_in,
           lambda_q1, lambda_k1, lambda_q2, lambda_k2, g_subln, w_branch_a, w_branch_b, w_gate, b_gate, w_out,
           g_ffn, w_router, b_router, w_up, b_up, w_down, b_down, g_final):
    b, n, d = x_prompt.shape
    bs, ns, _ = x_sample.shape
    past = cache_diff_k.shape[1]
    rows_p, rows_s = b * n, bs * ns
    assert n % SB_SEG == 0 and past % SB_SEG == 0

    w_in_bf = w_in.astype(BF16)
    lams = jnp.stack([lambda_q1, lambda_k1, lambda_q2, lambda_k2]).astype(F32)

    tile_p = _pick(n, 512)
    per_seq = n // tile_p
    p32 = _proj(x_prompt.reshape(rows_p, d), g_mix, w_in_bf, _rotary_tables(N_META + jnp.arange(n)),
                tile_p, lambda i: i % per_seq)
    tile_s = _pick(rows_s, 512)
    assert tile_s % ns == 0
    s32 = _proj(x_sample.reshape(rows_s, d), g_mix, w_in_bf,
                _rotary_tables(N_META + past + (jnp.arange(tile_s) % ns)), tile_s, lambda i: 0)
    m32 = _proj(meta_tokens, g_mix, w_in_bf, _rotary_tables(jnp.arange(N_META)), N_META, lambda i: i)
    _, _, _, _, qa, qb, ka, kb, va, vb = p32
    _, _, _, _, sqa, sqb, ska, skb, sva, svb = s32
    _, _, _, _, _, _, mka, mkb, mva, mvb = m32

    tq = _pick(n, 512)
    oa_p = _diff_prompt(qa, ka, va, mka, mva, lams, g_subln, b, n, tq, _pick(tq, 512))
    ob_p = _sb_prompt(qb, kb, vb, mkb, mvb, b, n, tq)
    oa_s, ob_s = _sample_attention(
        sqa, sqb, ska, sva, skb, svb,
        jnp.transpose(cache_diff_k, (0, 2, 3, 4, 1)).reshape(bs, 512, past), cache_diff_v.reshape(bs, past, 512),
        jnp.transpose(cache_sb_k, (0, 2, 3, 1)).reshape(bs, 512, past),
        jnp.transpose(cache_sb_v, (0, 2, 3, 1)).reshape(bs, 512, past),
        mka, mva, mkb, mvb, lams, g_subln, 0, bs, ns, past)

    dff = w_down.shape[1]
    w_glu, w_lin = _deinterleave(w_up)
    w_down_bf = _to_bf16(w_down)
    b_glu = b_up[:, 0::2].reshape(N_EXPERTS, 1, dff)
    b_lin = b_up[:, 1::2].reshape(N_EXPERTS, 1, dff)
    merge_w = (g_mix, w_gate.astype(BF16), b_gate, w_branch_a.astype(BF16), w_branch_b.astype(BF16),
               w_out.astype(BF16), g_ffn, w_router, b_router)

    def ffn(x_rows, oa, ob, tile):
        x2, h2, gates_t = _merge(x_rows, oa, ob, *merge_w, tile)
        return _moe(h2, gates_t, x2, w_glu, w_lin, b_glu, b_lin, w_down_bf, b_down.reshape(N_EXPERTS, 1, d),
                    g_final, _pick(x_rows.shape[0], 1024))

    y_p = ffn(x_prompt.reshape(rows_p, d), oa_p, ob_p, tile_p)
    y_s = ffn(x_sample.reshape(rows_s, d), oa_s, ob_s, tile_s)

    def with_meta(meta_rows, rows32, shape):
        full = jnp.concatenate([jnp.broadcast_to(meta_rows[None], (b, N_META, 512)), rows32.reshape(b, n, 512)], axis=1)
        return full.reshape((b, N_META + n) + shape)

    shapes = ((HA, 2, DHA), (HB, DHB), (HA, DVA), (HB, DHB))
    kv_p = [with_meta(m32[j], p32[j], shapes[j]) for j in range(4)]
    kv_s = [s32[j].reshape((bs, ns) + shapes[j]) for j in range(4)]
    return (y_p.reshape(b, n, d), y_s.reshape(bs, ns, d), kv_p[0], kv_p[2], kv_p[1], kv_p[3],
            kv_s[0], kv_s[2], kv_s[1], kv_s[3])
```

```python
import functools
import math

import jax
import jax.numpy as jnp
from jax import lax
from jax.experimental import pallas as pl
from jax.experimental.pallas import tpu as pltpu

F32 = jnp.float32
BF16 = jnp.bfloat16

N_META = 16
CHUNK = 64
HA = 4
DHA = 64
DVA = 128
HB = 8
DHB = 64
ROT_DIM = 16
ROPE_THETA = 500000.0
N_EXPERTS = 32
TOP_K = 4
SWIGLU_ALPHA = 1.702
SWIGLU_LIMIT = 7.0
EPS = 1e-5
LAM_INIT = 0.8 - 0.6 * math.exp(-0.3 * 0)
LOG2E = 1.4426950408889634

LANES = 128
NEG_BIG = -1e30
SB_SEG = 256
SB_DEAD = 150.0
VMEM_LIMIT = 56 * 1024 * 1024
MOE_CHUNK = 160


def _cparams(sem):
    return pltpu.CompilerParams(dimension_semantics=sem, vmem_limit_bytes=VMEM_LIMIT)


def _nt_dot(a, b):
    return lax.dot_general(a, b, (((1,), (1,)), ((), ())), preferred_element_type=F32)


def _dot(a, b):
    return jnp.dot(a, b, preferred_element_type=F32)


def _proj_kernel(x_ref, g_ref, w_ref, cos_ref, sa_ref, sb_ref,
                 ka32_ref, kb32_ref, va32_ref, vb32_ref,
                 qa_ref, qb_ref, ka_ref, kb_ref, va_ref, vb_ref):
    x = x_ref[...]
    ms = jnp.mean(x * x, axis=-1, keepdims=True)
    h = (x * lax.rsqrt(ms + EPS) * g_ref[...]).astype(BF16)
    proj = _dot(h, w_ref[...])
    cos = cos_ref[...]
    sa = sa_ref[...]
    sb = sb_ref[...]
    ones = jnp.ones((x.shape[0], LANES), BF16)

    def rot(c):
        return c * cos + pltpu.roll(c, 8, 1) * sa + pltpu.roll(c, LANES - 8, 1) * sb

    for j in range(4):
        lo = j * LANES
        qa = rot(proj[:, lo:lo + LANES]) * (DHA ** -0.5 * LOG2E)
        qa_ref[j] = qa.astype(BF16)
        qb = proj[:, 512 + lo:512 + lo + LANES] * (DHB ** -0.5 * LOG2E)
        qb_ref[j] = qb.astype(BF16)
        ka = rot(proj[:, 1024 + lo:1024 + lo + LANES])
        ka32_ref[:, lo:lo + LANES] = ka
        ka_ref[j] = ka.astype(BF16)
        kb = proj[:, 1536 + lo:1536 + lo + LANES]
        kb32_ref[:, lo:lo + LANES] = kb
        kb_ref[j] = kb.astype(BF16)
        va = proj[:, 2048 + lo:2048 + lo + LANES]
        va32_ref[:, lo:lo + LANES] = va
        va_ref[j, :, :LANES] = va.astype(BF16)
        va_ref[j, :, LANES:] = ones
        vb = proj[:, 2560 + lo:2560 + lo + LANES]
        vb32_ref[:, lo:lo + LANES] = vb
        vb_ref[j] = vb.astype(BF16)


def _proj(x, g_mix, w_in_bf, tabs, tile, tab_index):
    rows, d = x.shape
    n_tiles = rows // tile
    row_spec = lambda w: pl.BlockSpec((tile, w), lambda i: (i, 0))
    head_spec = lambda w: pl.BlockSpec((4, tile, w), lambda i: (0, i, 0))
    tab_spec = pl.BlockSpec((tile, LANES), lambda i: (tab_index(i), 0))
    const = lambda shape: pl.BlockSpec(shape, lambda i: (0,) * len(shape))
    f32_rows = jax.ShapeDtypeStruct((rows, 512), F32)
    bf_heads = jax.ShapeDtypeStruct((4, rows, LANES), BF16)
    return pl.pallas_call(
        _proj_kernel,
        grid=(n_tiles,),
        in_specs=[row_spec(d), const((1, d)), const(w_in_bf.shape), tab_spec, tab_spec, tab_spec],
        out_specs=[row_spec(512)] * 4 + [head_spec(LANES)] * 4 + [head_spec(2 * LANES), head_spec(LANES)],
        out_shape=[f32_rows] * 4 + [bf_heads] * 4
        + [jax.ShapeDtypeStruct((4, rows, 2 * LANES), BF16), bf_heads],
        compiler_params=_cparams(("parallel",)),
        name="proj",
    )(x, g_mix.reshape(1, d), w_in_bf, *tabs)


def _rotary_tables(pos):
    half = ROT_DIM // 2
    inv_freq = ROPE_THETA ** (-jnp.arange(0, ROT_DIM, 2, dtype=F32) / ROT_DIM)
    ang = pos.astype(F32)[:, None] * inv_freq[None, :]
    cos, sin = jnp.cos(ang), jnp.sin(ang)
    lane = jnp.arange(LANES) % DHA
    idx = lane % half
    first = (lane < half)[None, :]
    second = ((lane >= half) & (lane < ROT_DIM))[None, :]
    cos_t = jnp.where(first | second, cos[:, idx], 1.0)
    sa_t = jnp.where(second, sin[:, idx], 0.0)
    sb_t = jnp.where(first, -sin[:, idx], 0.0)
    return cos_t, sa_t, sb_t


def _stack_masked(q, split):
    lane = lax.broadcasted_iota(jnp.int32, q.shape, 1)
    zero = jnp.zeros_like(q)
    return jnp.concatenate([jnp.where(lane < split, q, zero), jnp.where(lane >= split, q, zero)], axis=0)


def _diff_scores(q2, k, k_is_t=False):
    return _dot(q2, k) if k_is_t else _nt_dot(q2, k)


def _diff_accumulate(s, vaug, m_ref, acc_ref, mask):
    if mask is not None:
        s = jnp.where(mask, s, NEG_BIG)
    m_prev = m_ref[...]
    m_new = jnp.maximum(m_prev, jnp.max(s, axis=-1, keepdims=True))
    alpha = jnp.exp2(m_prev - m_new)
    p = jnp.exp2(s - m_new[:, :1])
    pv = _dot(p.astype(BF16), vaug)
    acc_ref[...] = jnp.concatenate([alpha, alpha], axis=1) * acc_ref[...] + pv
    m_ref[...] = m_new


def _diff_update(q2, k, vaug, m_ref, acc_ref, mask, k_is_t=False):
    _diff_accumulate(_diff_scores(q2, k, k_is_t), vaug, m_ref, acc_ref, mask)


def _diff_finish(acc_ref, lam_ref, g_ref, t):
    lq1, lk1, lq2, lk2 = (lam_ref[i:i + 1, :] for i in range(4))
    lam = (jnp.exp(jnp.sum(lq1 * lk1, axis=-1, keepdims=True))
           - jnp.exp(jnp.sum(lq2 * lk2, axis=-1, keepdims=True)) + LAM_INIT)
    acc = acc_ref[...]
    o0 = acc[:t, :LANES] / acc[:t, LANES:]
    o1 = acc[t:, :LANES] / acc[t:, LANES:]
    o = o0 - lam * o1
    ms = jnp.mean(o * o, axis=-1, keepdims=True)
    return o * lax.rsqrt(ms + EPS) * g_ref[...] * (1.0 - LAM_INIT)


def _tri_incl(n):
    r = lax.broadcasted_iota(jnp.int32, (n, n), 0)
    c = lax.broadcasted_iota(jnp.int32, (n, n), 1)
    return jnp.where(r >= c, 1.0, 0.0).astype(BF16)


def _sb_update(q2, k, v, tri, carry_ref, acc_ref, mask, kv_is_t=False):
    z = _dot(q2, k) if kv_is_t else _nt_dot(q2, k)
    nlk = jnp.maximum(z, 0.0) + jnp.log2(1.0 + jnp.exp2(-jnp.abs(z)))
    if mask is not None:
        nlk = jnp.where(mask, nlk, 0.0)
    hi = nlk.astype(BF16)
    lo = (nlk - hi.astype(F32)).astype(BF16)
    seg = tri.shape[0]
    c = carry_ref[...][:, :1]
    args = []
    for j in reversed(range(z.shape[1] // seg)):
        sl = slice(j * seg, (j + 1) * seg)
        incl = _dot(hi[:, sl], tri) + _dot(lo[:, sl], tri)
        args.append(z[:, sl] - incl - c)
        c = c + incl[:, :1]
    arg = args[0] if len(args) == 1 else jnp.concatenate(args[::-1], axis=1)
    a = jnp.exp2(arg)
    if mask is not None:
        a = jnp.where(mask, a, 0.0)
    a = a.astype(BF16)
    acc_ref[...] += _nt_dot(a, v) if kv_is_t else _dot(a, v)
    carry_ref[...] = jnp.broadcast_to(c, carry_ref.shape)


def _sb_finish(acc_ref, t):
    acc = acc_ref[...]
    lane = lax.broadcasted_iota(jnp.int32, (t, LANES), 1)
    return jnp.where(lane < DHB, acc[:t], acc[t:])


def _diff_prompt_kernel(q_ref, k_ref, v_ref, mk_ref, mv_ref, lam_ref, g_ref, o_ref,
                        m_ref, acc_ref, s_ref, *, t, nh):
    i = pl.program_id(2)
    q2 = [_stack_masked(q_ref[h], DHA) for h in range(nh)]
    m_ref[...] = jnp.full(m_ref.shape, NEG_BIG, F32)
    acc_ref[...] = jnp.zeros(acc_ref.shape, F32)
    for h in range(nh):
        s_ref[h] = _diff_scores(q2[h], k_ref[h, 0:t, :])
        _diff_update(q2[h], mk_ref[h], mv_ref[h], m_ref.at[h], acc_ref.at[h], None)

    def body(j, c):
        cur = pl.multiple_of(j * t, t)
        nxt = pl.multiple_of((j + 1) * t, t)
        for h in range(nh):
            _diff_accumulate(s_ref[h], v_ref[h, pl.ds(cur, t), :], m_ref.at[h], acc_ref.at[h], None)
            s_ref[h] = _diff_scores(q2[h], k_ref[h, pl.ds(nxt, t), :])
        return c

    lax.fori_loop(0, i, body, 0)

    row = lax.broadcasted_iota(jnp.int32, (t, t), 0)
    col = lax.broadcasted_iota(jnp.int32, (t, t), 1)
    mask = (col // CHUNK) <= (row // CHUNK)
    mask2 = jnp.concatenate([mask, mask], axis=0)
    diag = pl.multiple_of(i * t, t)
    for h in range(nh):
        _diff_accumulate(s_ref[h], v_ref[h, pl.ds(diag, t), :], m_ref.at[h], acc_ref.at[h], mask2)
        o_ref[:, h * LANES:(h + 1) * LANES] = _diff_finish(acc_ref.at[h], lam_ref, g_ref, t).astype(o_ref.dtype)


def _diff_prompt(qa, ka, va, mka, mva, lams, g_subln, batch, seq, tq, nh=4):
    nq = seq // tq
    kern = functools.partial(_diff_prompt_kernel, t=tq, nh=nh)
    return pl.pallas_call(
        kern,
        grid=(batch, HA // nh, nq),
        in_specs=[
            pl.BlockSpec((nh, tq, LANES), lambda b, h, i: (h, b * nq + i, 0)),
            pl.BlockSpec((nh, seq, LANES), lambda b, h, i: (h, b, 0), pipeline_mode=pl.Buffered(1)),
            pl.BlockSpec((nh, seq, 2 * LANES), lambda b, h, i: (h, b, 0), pipeline_mode=pl.Buffered(1)),
            pl.BlockSpec((nh, N_META, LANES), lambda b, h, i: (h, 0, 0)),
            pl.BlockSpec((nh, N_META, 2 * LANES), lambda b, h, i: (h, 0, 0)),
            pl.BlockSpec((4, DHA), lambda b, h, i: (0, 0)),
            pl.BlockSpec((1, DVA), lambda b, h, i: (0, 0)),
        ],
        out_specs=pl.BlockSpec((tq, nh * LANES), lambda b, h, i: (b * nq + i, h)),
        out_shape=jax.ShapeDtypeStruct((batch * seq, HA * DVA), BF16),
        scratch_shapes=[pltpu.VMEM((nh, 2 * tq, LANES), F32), pltpu.VMEM((nh, 2 * tq, 2 * LANES), F32),
                        pltpu.VMEM((nh, 2 * tq, tq), F32)],
        compiler_params=_cparams(("parallel", "parallel", "arbitrary")),
        name="diff_prompt",
    )(qa, ka, va, mka, mva, lams, g_subln.reshape(1, DVA))


def _sb_live(carry_ref):
    return (jnp.min(carry_ref[...]) < SB_DEAD).astype(jnp.int32)


def _sb_prompt_kernel(q_ref, k_ref, v_ref, mk_ref, mv_ref, o_ref, carry_ref, acc_ref, *, tq):
    i = pl.program_id(2)
    seg = SB_SEG
    per_q = tq // seg
    q2 = _stack_masked(q_ref[0], DHB)
    tri = _tri_incl(seg)
    carry_ref[...] = jnp.zeros(carry_ref.shape, F32)
    acc_ref[...] = jnp.zeros(acc_ref.shape, F32)

    row = lax.broadcasted_iota(jnp.int32, (tq, tq), 0)
    col = lax.broadcasted_iota(jnp.int32, (tq, tq), 1)
    strict = col < row
    start = pl.multiple_of(i * tq, tq)
    _sb_update(q2, k_ref[0, pl.ds(start, tq), :], v_ref[0, pl.ds(start, tq), :], tri,
               carry_ref, acc_ref, jnp.concatenate([strict, strict], axis=0))

    n_full = i * per_q

    def cond(state):
        t, live = state
        return jnp.logical_and(t < n_full, live > 0)

    def body(state):
        t, _ = state
        start = pl.multiple_of((n_full - 1 - t) * seg, seg)
        _sb_update(q2, k_ref[0, pl.ds(start, seg), :], v_ref[0, pl.ds(start, seg), :], tri,
                   carry_ref, acc_ref, None)
        return t + 1, _sb_live(carry_ref)

    lax.while_loop(cond, body, (jnp.int32(0), _sb_live(carry_ref)))

    @pl.when(_sb_live(carry_ref) > 0)
    def _():
        _sb_update(q2, mk_ref[0], mv_ref[0], tri[:N_META, :N_META], carry_ref, acc_ref, None)

    o_ref[...] = _sb_finish(acc_ref, tq).astype(o_ref.dtype)


def _sb_prompt(qb, kb, vb, mkb, mvb, batch, seq, tq):
    nq = seq // tq
    kern = functools.partial(_sb_prompt_kernel, tq=tq)
    return pl.pallas_call(
        kern,
        grid=(batch, HB // 2, nq),
        in_specs=[
            pl.BlockSpec((1, tq, LANES), lambda b, h, i: (h, b * nq + i, 0)),
            pl.BlockSpec((1, seq, LANES), lambda b, h, i: (h, b, 0)),
            pl.BlockSpec((1, seq, LANES), lambda b, h, i: (h, b, 0)),
            pl.BlockSpec((1, N_META, LANES), lambda b, h, i: (h, 0, 0)),
            pl.BlockSpec((1, N_META, LANES), lambda b, h, i: (h, 0, 0)),
        ],
        out_specs=pl.BlockSpec((tq, LANES), lambda b, h, i: (b * nq + i, h)),
        out_shape=jax.ShapeDtypeStruct((batch * seq, HB * DHB), BF16),
        scratch_shapes=[pltpu.VMEM((2 * tq, LANES), F32), pltpu.VMEM((2 * tq, LANES), F32)],
        compiler_params=_cparams(("parallel", "parallel", "arbitrary")),
        name="sb_prompt",
    )(qb, kb, vb, mkb, mvb)


def _sample_kernel(qa_ref, qb_ref, nka_ref, nva_ref, nkb_ref, nvb_ref,
                   cka_ref, cva_ref, ckb_ref, cvb_ref,
                   mka_ref, mva_ref, mkb_ref, mvb_ref, lam_ref, g_ref,
                   oa_ref, ob_ref, m_ref, acca_ref, carry_ref, accb_ref, *, ns, past):
    seg = SB_SEG
    n_seg = past // seg
    row = lax.broadcasted_iota(jnp.int32, (ns, ns), 0)
    col = lax.broadcasted_iota(jnp.int32, (ns, ns), 1)

    q2 = _stack_masked(qa_ref[0], DHA)
    m_ref[...] = jnp.full(m_ref.shape, NEG_BIG, F32)
    acca_ref[...] = jnp.zeros(acca_ref.shape, F32)
    _diff_update(q2, mka_ref[0], mva_ref[0], m_ref, acca_ref, None)
    cache_v = jnp.concatenate([cva_ref[0].astype(BF16), jnp.ones((past, LANES), BF16)], axis=1)
    _diff_update(q2, cka_ref[0].astype(BF16), cache_v, m_ref, acca_ref, None, k_is_t=True)
    mask = ((past + col) // CHUNK) <= ((past + row) // CHUNK)
    _diff_update(q2, nka_ref[0], nva_ref[0], m_ref, acca_ref, jnp.concatenate([mask, mask], axis=0))
    oa_ref[...] = _diff_finish(acca_ref, lam_ref, g_ref, ns).astype(oa_ref.dtype)

    q2 = _stack_masked(qb_ref[0], DHB)
    tri = _tri_incl(seg)
    carry_ref[...] = jnp.zeros(carry_ref.shape, F32)
    accb_ref[...] = jnp.zeros(accb_ref.shape, F32)
    strict = col < row
    _sb_update(q2, nkb_ref[0], nvb_ref[0], tri[:ns, :ns], carry_ref, accb_ref,
               jnp.concatenate([strict, strict], axis=0))
    for s in reversed(range(n_seg)):
        @pl.when(_sb_live(carry_ref) > 0)
        def _(s=s):
            k = ckb_ref[0, :, s * seg:(s + 1) * seg].astype(BF16)
            v = cvb_ref[0, :, s * seg:(s + 1) * seg].astype(BF16)
            _sb_update(q2, k, v, tri, carry_ref, accb_ref, None, kv_is_t=True)

    @pl.when(_sb_live(carry_ref) > 0)
    def _():
        _sb_update(q2, mkb_ref[0], mvb_ref[0], tri[:N_META, :N_META], carry_ref, accb_ref, None)

    ob_ref[...] = _sb_finish(accb_ref, ns).astype(ob_ref.dtype)


def _sample_attention(qa, qb, ka, va, kb, vb, cka, cva, ckb, cvb, mka, mva, mkb, mvb, lams, g_subln,
                      row0, nb, ns, past):
    blk0 = row0 // ns
    new = lambda w: pl.BlockSpec((1, ns, w), lambda b, h: (h, blk0 + b, 0))
    cache = pl.BlockSpec((1, past, LANES), lambda b, h: (b, 0, h))
    cache_t = pl.BlockSpec((1, LANES, past), lambda b, h: (b, h, 0))
    meta = lambda w: pl.BlockSpec((1, N_META, w), lambda b, h: (h, 0, 0))
    out = pl.BlockSpec((ns, LANES), lambda b, h: (b, h))
    kern = functools.partial(_sample_kernel, ns=ns, past=past)
    return pl.pallas_call(
        kern,
        grid=(nb, 4),
        in_specs=[new(LANES), new(LANES), new(LANES), new(2 * LANES), new(LANES), new(LANES),
                  cache_t, cache, cache_t, cache_t,
                  meta(LANES), meta(2 * LANES), meta(LANES), meta(LANES),
                  pl.BlockSpec((4, DHA), lambda b, h: (0, 0)),
                  pl.BlockSpec((1, DVA), lambda b, h: (0, 0))],
        out_specs=[out, out],
        out_shape=[jax.ShapeDtypeStruct((nb * ns, 512), BF16)] * 2,
        scratch_shapes=[pltpu.VMEM((2 * ns, LANES), F32), pltpu.VMEM((2 * ns, 2 * LANES), F32),
                        pltpu.VMEM((2 * ns, LANES), F32), pltpu.VMEM((2 * ns, LANES), F32)],
        compiler_params=_cparams(("parallel", "parallel")),
        name="sample_attn",
    )(qa, qb, ka, va, kb, vb, cka, cva, ckb, cvb, mka, mva, mkb, mvb, lams, g_subln.reshape(1, DVA))


def _merge_kernel(x_ref, oa_ref, ob_ref, gmix_ref, wg_ref, bg_ref, wa_ref, wb_ref, wo_ref,
                  gffn_ref, wrt_ref, br_ref, x2_ref, h2_ref, gates_ref):
    x = x_ref[...]
    d = x.shape[1]
    ms = jnp.mean(x * x, axis=-1, keepdims=True)
    h = (x * lax.rsqrt(ms + EPS) * gmix_ref[...]).astype(BF16)
    gate = jax.nn.sigmoid(_dot(h, wg_ref[...]) + bg_ref[...])
    merged = gate[:, :d] * _dot(oa_ref[...], wa_ref[...]) + gate[:, d:] * _dot(ob_ref[...], wb_ref[...])
    x2 = x + _dot(merged.astype(BF16), wo_ref[...])
    x2_ref[...] = x2
    ms2 = jnp.mean(x2 * x2, axis=-1, keepdims=True)
    h2 = x2 * lax.rsqrt(ms2 + EPS) * gffn_ref[...]
    h2_ref[...] = h2.astype(BF16)

    h_hi = h2.astype(BF16)
    h_lo = (h2 - h_hi.astype(F32)).astype(BF16)
    wr = wrt_ref[...]
    w_hi = wr.astype(BF16)
    w_lo = (wr - w_hi.astype(F32)).astype(BF16)
    logits = _nt_dot(w_hi, h_hi) + _nt_dot(w_hi, h_lo) + _nt_dot(w_lo, h_hi) + br_ref[...]

    sub = lax.broadcasted_iota(jnp.int32, logits.shape, 0).astype(F32)
    work = logits
    picks = []
    for _ in range(TOP_K):
        top = jnp.max(work, axis=0, keepdims=True)
        first = jnp.min(jnp.where(work == top, sub, float(N_EXPERTS)), axis=0, keepdims=True)
        hit = sub == first
        picks.append((top, hit))
        work = jnp.where(hit, -jnp.inf, work)
    exps = [jnp.exp(v - picks[0][0]) for v, _ in picks]
    denom = exps[0] + exps[1] + exps[2] + exps[3]
    gates = jnp.zeros(logits.shape, F32)
    for e, (_, hit) in zip(exps, picks):
        gates = gates + jnp.where(hit, e / denom, 0.0)
    gates_ref[...] = gates


def _merge(x, oa, ob, g_mix, w_gate, b_gate, w_a, w_b, w_out, g_ffn, w_router, b_router, tile):
    rows, d = x.shape
    row = lambda w: pl.BlockSpec((tile, w), lambda i: (i, 0))
    const = lambda shape: pl.BlockSpec(shape, lambda i: (0,) * len(shape))
    return pl.pallas_call(
        _merge_kernel,
        grid=(rows // tile,),
        in_specs=[row(d), row(512), row(512), const((1, d)), const((d, 2 * d)), const((1, 2 * d)),
                  const((512, d)), const((512, d)), const((d, d)), const((1, d)),
                  const((N_EXPERTS, d)), const((N_EXPERTS, 1))],
        out_specs=[row(d), row(d), pl.BlockSpec((N_EXPERTS, tile), lambda i: (0, i))],
        out_shape=[jax.ShapeDtypeStruct((rows, d), F32), jax.ShapeDtypeStruct((rows, d), BF16),
                   jax.ShapeDtypeStruct((N_EXPERTS, rows), F32)],
        compiler_params=_cparams(("parallel",)),
        name="merge",
    )(x, oa, ob, g_mix.reshape(1, d), w_gate, b_gate.reshape(1, 2 * d), w_a, w_b, w_out, g_ffn.reshape(1, d),
      w_router.T, b_router.reshape(N_EXPERTS, 1))


def _deinterleave_kernel(w_ref, glu_ref, lin_ref):
    r = lax.broadcasted_iota(jnp.int32, (2 * LANES, 2 * LANES), 0)
    c = lax.broadcasted_iota(jnp.int32, (2 * LANES, 2 * LANES), 1)
    src = jnp.where(c < LANES, 2 * c, 2 * (c - LANES) + 1)
    sel = jnp.where(r == src, 1.0, 0.0).astype(BF16)
    for j in range(w_ref.shape[2] // (2 * LANES)):
        chunk = w_ref[0, :, j * 2 * LANES:(j + 1) * 2 * LANES].astype(BF16)
        both = _dot(chunk, sel)
        glu_ref[0, :, j * LANES:(j + 1) * LANES] = both[:, :LANES].astype(BF16)
        lin_ref[0, :, j * LANES:(j + 1) * LANES] = both[:, LANES:].astype(BF16)


def _deinterleave(w_up):
    ne, d, two_f = w_up.shape
    cols = two_f
    out = jax.ShapeDtypeStruct((ne, d, two_f // 2), BF16)
    return pl.pallas_call(
        _deinterleave_kernel,
        grid=(ne, two_f // cols),
        in_specs=[pl.BlockSpec((1, d, cols), lambda e, j: (e, 0, j))],
        out_specs=[pl.BlockSpec((1, d, cols // 2), lambda e, j: (e, 0, j))] * 2,
        out_shape=[out, out],
        compiler_params=_cparams(("parallel", "parallel")),
        name="deinterleave",
    )(w_up)


def _to_bf16_kernel(w_ref, o_ref):
    o_ref[...] = w_ref[...].astype(BF16)


def _to_bf16(w):
    ne, a, b = w.shape
    spec = pl.BlockSpec((1, a, b), lambda e: (e, 0, 0))
    return pl.pallas_call(
        _to_bf16_kernel, grid=(ne,), in_specs=[spec], out_specs=spec,
        out_shape=jax.ShapeDtypeStruct(w.shape, BF16), compiler_params=_cparams(("parallel",)), name="to_bf16",
    )(w)


def _moe_kernel(h_ref, gt_ref, x2_ref, tri_ref, wg_ref, wl_ref, bg_ref, bl_ref, wd_ref, bd_ref, gf_ref,
                y_ref, rank_ref, acc_ref):
    e = pl.program_id(1)
    tb = h_ref.shape[0]
    ch = MOE_CHUNK

    @pl.when(e == 0)
    def _():
        acc_ref[...] = jnp.zeros(acc_ref.shape, F32)
        member = jnp.where(gt_ref[...] > 0.0, 1.0, 0.0).astype(BF16)
        rank_ref[...] = _dot(member, tri_ref[...])

    gate_e = gt_ref[pl.ds(e, 1), :]
    routed = gate_e > 0.0
    key = jnp.where(routed, rank_ref[pl.ds(e, 1), :], -1.0)
    n = jnp.sum(jnp.where(routed, 1.0, 0.0)).astype(jnp.int32)

    def body(c, carry):
        slot = (lax.broadcasted_iota(jnp.int32, (ch, tb), 0) + c * ch).astype(F32)
        sel = key == slot
        g = jnp.where(sel, 1.0, 0.0).astype(BF16)
        xg = _dot(g, h_ref[...]).astype(BF16)
        glu = jnp.minimum(_dot(xg, wg_ref[0]) + bg_ref[0], SWIGLU_LIMIT)
        lin = jnp.clip(_dot(xg, wl_ref[0]) + bl_ref[0], -SWIGLU_LIMIT, SWIGLU_LIMIT)
        act = glu * jax.nn.sigmoid(SWIGLU_ALPHA * glu) * (lin + 1.0)
        y = _dot(act.astype(BF16), wd_ref[0]) + bd_ref[0]
        w = jnp.sum(jnp.where(sel, gate_e, 0.0), axis=1, keepdims=True)
        ys = (y * w).astype(BF16)
        acc_ref[...] += lax.dot_general(g, ys, (((0,), (0,)), ((), ())), preferred_element_type=F32)
        return carry

    lax.fori_loop(0, (n + ch - 1) // ch, body, 0)

    @pl.when(e == pl.num_programs(1) - 1)
    def _():
        x3 = x2_ref[...] + acc_ref[...]
        ms = jnp.mean(x3 * x3, axis=-1, keepdims=True)
        y_ref[...] = x3 * lax.rsqrt(ms + EPS) * gf_ref[...]


def _moe(h2, gates_t, x2, w_glu, w_lin, b_glu, b_lin, w_down, b_down, g_final, tile):
    rows, d = x2.shape
    dff = w_glu.shape[2]
    row = lambda w: pl.BlockSpec((tile, w), lambda i, e: (i, 0))
    exp = lambda a, b: pl.BlockSpec((1, a, b), lambda i, e: (e, 0, 0))
    r = lax.broadcasted_iota(jnp.int32, (tile, tile), 0)
    c = lax.broadcasted_iota(jnp.int32, (tile, tile), 1)
    tri = jnp.where(r < c, 1.0, 0.0).astype(BF16)
    return pl.pallas_call(
        _moe_kernel,
        grid=(rows // tile, N_EXPERTS),
        in_specs=[row(d), pl.BlockSpec((N_EXPERTS, tile), lambda i, e: (0, i)), row(d),
                  pl.BlockSpec((tile, tile), lambda i, e: (0, 0)),
                  exp(d, dff), exp(d, dff), exp(1, dff), exp(1, dff),
                  exp(dff, d), exp(1, d), pl.BlockSpec((1, d), lambda i, e: (0, 0))],
        out_specs=row(d),
        out_shape=jax.ShapeDtypeStruct((rows, d), F32),
        scratch_shapes=[pltpu.VMEM((N_EXPERTS, tile), F32), pltpu.VMEM((tile, d), F32)],
        compiler_params=_cparams(("parallel", "arbitrary")),
        name="moe",
    )(h2, gates_t, x2, tri, w_glu, w_lin, b_glu, b_lin, w_down, b_down, g_final.reshape(1, d))


def _pick(n, pref):
    t = min(n, pref)
    assert n % t == 0, (n, pref)
    return t


def kernel(x_prompt, x_sample, cache_diff_k, cache_diff_v, cache_sb_k, cache_sb_v, meta_tokens, g_mix, w_in,
           lambda_q1, lambda_k1, lambda_q2, lambda_k2, g_subln, w_branch_a, w_branch_b, w_gate, b_gate, w_out,
           g_ffn, w_router, b_router, w_up, b_up, w_down, b_down, g_final):
    b, n, d = x_prompt.shape
    bs, ns, _ = x_sample.shape
    past = cache_diff_k.shape[1]
    rows_p, rows_s = b * n, bs * ns
    assert n % SB_SEG == 0 and past % SB_SEG == 0

    w_in_bf = w_in.astype(BF16)
    lams = jnp.stack([lambda_q1, lambda_k1, lambda_q2, lambda_k2]).astype(F32)

    tile_p = _pick(n, 512)
    per_seq = n // tile_p
    p32 = _proj(x_prompt.reshape(rows_p, d), g_mix, w_in_bf, _rotary_tables(N_META + jnp.arange(n)),
                tile_p, lambda i: i % per_seq)
    tile_s = _pick(rows_s, 512)
    assert tile_s % ns == 0
    s32 = _proj(x_sample.reshape(rows_s, d), g_mix, w_in_bf,
                _rotary_tables(N_META + past + (jnp.arange(tile_s) % ns)), tile_s, lambda i: 0)
    m32 = _proj(meta_tokens, g_mix, w_in_bf, _rotary_tables(jnp.arange(N_META)), N_META, lambda i: i)
    _, _, _, _, qa, qb, ka, kb, va, vb = p32
    _, _, _, _, sqa, sqb, ska, skb, sva, svb = s32
    _, _, _, _, _, _, mka, mkb, mva, mvb = m32

    tq = _pick(n, 512)
    oa_p = _diff_prompt(qa, ka, va, mka, mva, lams, g_subln, b, n, tq)
    ob_p = _sb_prompt(qb, kb, vb, mkb, mvb, b, n, tq)
    oa_s, ob_s = _sample_attention(
        sqa, sqb, ska, sva, skb, svb,
        jnp.transpose(cache_diff_k, (0, 2, 3, 4, 1)).reshape(bs, 512, past), cache_diff_v.reshape(bs, past, 512),
        jnp.transpose(cache_sb_k, (0, 2, 3, 1)).reshape(bs, 512, past),
        jnp.transpose(cache_sb_v, (0, 2, 3, 1)).reshape(bs, 512, past),
        mka, mva, mkb, mvb, lams, g_subln, 0, bs, ns, past)

    dff = w_down.shape[1]
    w_glu, w_lin = _deinterleave(w_up)
    w_down_bf = _to_bf16(w_down)
    b_glu = b_up[:, 0::2].reshape(N_EXPERTS, 1, dff)
    b_lin = b_up[:, 1::2].reshape(N_EXPERTS, 1, dff)
    merge_w = (g_mix, w_gate.astype(BF16), b_gate, w_branch_a.astype(BF16), w_branch_b.astype(BF16),
               w_out.astype(BF16), g_ffn, w_router, b_router)

    def ffn(x_rows, oa, ob, tile):
        x2, h2, gates_t = _merge(x_rows, oa, ob, *merge_w, tile)
        return _moe(h2, gates_t, x2, w_glu, w_lin, b_glu, b_lin, w_down_bf, b_down.reshape(N_EXPERTS, 1, d),
                    g_final, _pick(x_rows.shape[0], 1024))

    y_p = ffn(x_prompt.reshape(rows_p, d), oa_p, ob_p, tile_p)
    y_s = ffn(x_sample.reshape(rows_s, d), oa_s, ob_s, tile_s)

    def with_meta(meta_rows, rows32, shape):
        full = jnp.concatenate([jnp.broadcast_to(meta_rows[None], (b, N_META, 512)), rows32.reshape(b, n, 512)], axis=1)
        return full.reshape((b, N_META + n) + shape)

    shapes = ((HA, 2, DHA), (HB, DHB), (HA, DVA), (HB, DHB))
    kv_p = [with_meta(m32[j], p32[j], shapes[j]) for j in range(4)]
    kv_s = [s32[j].reshape((bs, ns) + shapes[j]) for j in range(4)]
    return (y_p.reshape(b, n, d), y_s.reshape(bs, ns, d), kv_p[0], kv_p[2], kv_p[1], kv_p[3],
            kv_s[0], kv_s[2], kv_s[1], kv_s[3])
```

```python
import functools
import math

import jax
import jax.numpy as jnp
from jax import lax
from jax.experimental import pallas as pl
from jax.experimental.pallas import tpu as pltpu

F32 = jnp.float32
BF16 = jnp.bfloat16

N_META = 16
CHUNK = 64
HA = 4
DHA = 64
DVA = 128
HB = 8
DHB = 64
ROT_DIM = 16
ROPE_THETA = 500000.0
N_EXPERTS = 32
TOP_K = 4
SWIGLU_ALPHA = 1.702
SWIGLU_LIMIT = 7.0
EPS = 1e-5
LAM_INIT = 0.8 - 0.6 * math.exp(-0.3 * 0)
LOG2E = 1.4426950408889634

LANES = 128
NEG_BIG = -1e30
SB_SEG = 256
SB_DEAD = 150.0
VMEM_LIMIT = 56 * 1024 * 1024
MOE_SUB = 1024
MOE_CHUNK = 160


def _cparams(sem):
    return pltpu.CompilerParams(dimension_semantics=sem, vmem_limit_bytes=VMEM_LIMIT)


def _nt_dot(a, b):
    return lax.dot_general(a, b, (((1,), (1,)), ((), ())), preferred_element_type=F32)


def _dot(a, b):
    return jnp.dot(a, b, preferred_element_type=F32)


def _proj_kernel(x_ref, g_ref, w_ref, cos_ref, sa_ref, sb_ref,
                 ka32_ref, kb32_ref, va32_ref, vb32_ref,
                 qa_ref, qb_ref, ka_ref, kb_ref, va_ref, vb_ref):
    x = x_ref[...]
    ms = jnp.mean(x * x, axis=-1, keepdims=True)
    h = (x * lax.rsqrt(ms + EPS) * g_ref[...]).astype(BF16)
    proj = _dot(h, w_ref[...])
    cos = cos_ref[...]
    sa = sa_ref[...]
    sb = sb_ref[...]
    ones = jnp.ones((x.shape[0], LANES), BF16)

    def rot(c):
        return c * cos + pltpu.roll(c, 8, 1) * sa + pltpu.roll(c, LANES - 8, 1) * sb

    for j in range(4):
        lo = j * LANES
        qa = rot(proj[:, lo:lo + LANES]) * (DHA ** -0.5 * LOG2E)
        qa_ref[j] = qa.astype(BF16)
        qb = proj[:, 512 + lo:512 + lo + LANES] * (DHB ** -0.5 * LOG2E)
        qb_ref[j] = qb.astype(BF16)
        ka = rot(proj[:, 1024 + lo:1024 + lo + LANES])
        ka32_ref[:, lo:lo + LANES] = ka
        ka_ref[j] = ka.astype(BF16)
        kb = proj[:, 1536 + lo:1536 + lo + LANES]
        kb32_ref[:, lo:lo + LANES] = kb
        kb_ref[j] = kb.astype(BF16)
        va = proj[:, 2048 + lo:2048 + lo + LANES]
        va32_ref[:, lo:lo + LANES] = va
        va_ref[j, :, :LANES] = va.astype(BF16)
        va_ref[j, :, LANES:] = ones
        vb = proj[:, 2560 + lo:2560 + lo + LANES]
        vb32_ref[:, lo:lo + LANES] = vb
        vb_ref[j] = vb.astype(BF16)


def _proj(x, g_mix, w_in_bf, tabs, tile, tab_index):
    rows, d = x.shape
    n_tiles = rows // tile
    row_spec = lambda w: pl.BlockSpec((tile, w), lambda i: (i, 0))
    head_spec = lambda w: pl.BlockSpec((4, tile, w), lambda i: (0, i, 0))
    tab_spec = pl.BlockSpec((tile, LANES), lambda i: (tab_index(i), 0))
    const = lambda shape: pl.BlockSpec(shape, lambda i: (0,) * len(shape))
    f32_rows = jax.ShapeDtypeStruct((rows, 512), F32)
    bf_heads = jax.ShapeDtypeStruct((4, rows, LANES), BF16)
    return pl.pallas_call(
        _proj_kernel,
        grid=(n_tiles,),
        in_specs=[row_spec(d), const((1, d)), const(w_in_bf.shape), tab_spec, tab_spec, tab_spec],
        out_specs=[row_spec(512)] * 4 + [head_spec(LANES)] * 4 + [head_spec(2 * LANES), head_spec(LANES)],
        out_shape=[f32_rows] * 4 + [bf_heads] * 4
        + [jax.ShapeDtypeStruct((4, rows, 2 * LANES), BF16), bf_heads],
        compiler_params=_cparams(("parallel",)),
        name="proj",
    )(x, g_mix.reshape(1, d), w_in_bf, *tabs)


def _rotary_tables(pos):
    half = ROT_DIM // 2
    inv_freq = ROPE_THETA ** (-jnp.arange(0, ROT_DIM, 2, dtype=F32) / ROT_DIM)
    ang = pos.astype(F32)[:, None] * inv_freq[None, :]
    cos, sin = jnp.cos(ang), jnp.sin(ang)
    lane = jnp.arange(LANES) % DHA
    idx = lane % half
    first = (lane < half)[None, :]
    second = ((lane >= half) & (lane < ROT_DIM))[None, :]
    cos_t = jnp.where(first | second, cos[:, idx], 1.0)
    sa_t = jnp.where(second, sin[:, idx], 0.0)
    sb_t = jnp.where(first, -sin[:, idx], 0.0)
    return cos_t, sa_t, sb_t


def _stack_masked(q, split):
    lane = lax.broadcasted_iota(jnp.int32, q.shape, 1)
    zero = jnp.zeros_like(q)
    return jnp.concatenate([jnp.where(lane < split, q, zero), jnp.where(lane >= split, q, zero)], axis=0)


def _diff_scores(q2, k, k_is_t=False):
    return _dot(q2, k) if k_is_t else _nt_dot(q2, k)


def _diff_accumulate(s, vaug, m_ref, acc_ref, mask):
    if mask is not None:
        s = jnp.where(mask, s, NEG_BIG)
    m_prev = m_ref[...]
    m_new = jnp.maximum(m_prev, jnp.max(s, axis=-1, keepdims=True))
    alpha = jnp.exp2(m_prev - m_new)
    p = jnp.exp2(s - m_new[:, :1])
    pv = _dot(p.astype(BF16), vaug)
    acc_ref[...] = jnp.concatenate([alpha, alpha], axis=1) * acc_ref[...] + pv
    m_ref[...] = m_new


def _diff_update(q2, k, vaug, m_ref, acc_ref, mask, k_is_t=False):
    _diff_accumulate(_diff_scores(q2, k, k_is_t), vaug, m_ref, acc_ref, mask)


def _diff_finish(acc_ref, lam_ref, g_ref, t):
    lq1, lk1, lq2, lk2 = (lam_ref[i:i + 1, :] for i in range(4))
    lam = (jnp.exp(jnp.sum(lq1 * lk1, axis=-1, keepdims=True))
           - jnp.exp(jnp.sum(lq2 * lk2, axis=-1, keepdims=True)) + LAM_INIT)
    acc = acc_ref[...]
    o0 = acc[:t, :LANES] / acc[:t, LANES:]
    o1 = acc[t:, :LANES] / acc[t:, LANES:]
    o = o0 - lam * o1
    ms = jnp.mean(o * o, axis=-1, keepdims=True)
    return o * lax.rsqrt(ms + EPS) * g_ref[...] * (1.0 - LAM_INIT)


def _tri_incl(n):
    r = lax.broadcasted_iota(jnp.int32, (n, n), 0)
    c = lax.broadcasted_iota(jnp.int32, (n, n), 1)
    return jnp.where(r >= c, 1.0, 0.0).astype(BF16)


def _sb_update(q2, k, v, tri, carry_ref, acc_ref, mask, kv_is_t=False):
    z = _dot(q2, k) if kv_is_t else _nt_dot(q2, k)
    nlk = jnp.maximum(z, 0.0) + jnp.log2(1.0 + jnp.exp2(-jnp.abs(z)))
    if mask is not None:
        nlk = jnp.where(mask, nlk, 0.0)
    hi = nlk.astype(BF16)
    lo = (nlk - hi.astype(F32)).astype(BF16)
    seg = tri.shape[0]
    c = carry_ref[...][:, :1]
    args = []
    for j in reversed(range(z.shape[1] // seg)):
        sl = slice(j * seg, (j + 1) * seg)
        incl = _dot(hi[:, sl], tri) + _dot(lo[:, sl], tri)
        args.append(z[:, sl] - incl - c)
        c = c + incl[:, :1]
    arg = args[0] if len(args) == 1 else jnp.concatenate(args[::-1], axis=1)
    a = jnp.exp2(arg)
    if mask is not None:
        a = jnp.where(mask, a, 0.0)
    a = a.astype(BF16)
    acc_ref[...] += _nt_dot(a, v) if kv_is_t else _dot(a, v)
    carry_ref[...] = jnp.broadcast_to(c, carry_ref.shape)


def _sb_finish(acc_ref, t):
    acc = acc_ref[...]
    lane = lax.broadcasted_iota(jnp.int32, (t, LANES), 1)
    return jnp.where(lane < DHB, acc[:t], acc[t:])


def _diff_prompt_kernel(q_ref, k_ref, v_ref, mk_ref, mv_ref, lam_ref, g_ref, o_ref,
                        m_ref, acc_ref, s_ref, *, t, nh):
    i = pl.program_id(2)
    q2 = [_stack_masked(q_ref[h], DHA) for h in range(nh)]
    m_ref[...] = jnp.full(m_ref.shape, NEG_BIG, F32)
    acc_ref[...] = jnp.zeros(acc_ref.shape, F32)
    for h in range(nh):
        s_ref[h] = _diff_scores(q2[h], k_ref[h, 0:t, :])
        _diff_update(q2[h], mk_ref[h], mv_ref[h], m_ref.at[h], acc_ref.at[h], None)

    def body(j, c):
        cur = pl.multiple_of(j * t, t)
        nxt = pl.multiple_of((j + 1) * t, t)
        for h in range(nh):
            _diff_accumulate(s_ref[h], v_ref[h, pl.ds(cur, t), :], m_ref.at[h], acc_ref.at[h], None)
            s_ref[h] = _diff_scores(q2[h], k_ref[h, pl.ds(nxt, t), :])
        return c

    lax.fori_loop(0, i, body, 0)

    row = lax.broadcasted_iota(jnp.int32, (t, t), 0)
    col = lax.broadcasted_iota(jnp.int32, (t, t), 1)
    mask = (col // CHUNK) <= (row // CHUNK)
    mask2 = jnp.concatenate([mask, mask], axis=0)
    diag = pl.multiple_of(i * t, t)
    for h in range(nh):
        _diff_accumulate(s_ref[h], v_ref[h, pl.ds(diag, t), :], m_ref.at[h], acc_ref.at[h], mask2)
        o_ref[:, h * LANES:(h + 1) * LANES] = _diff_finish(acc_ref.at[h], lam_ref, g_ref, t).astype(o_ref.dtype)


def _diff_prompt(qa, ka, va, mka, mva, lams, g_subln, batch, seq, tq, nh=4):
    nq = seq // tq
    kern = functools.partial(_diff_prompt_kernel, t=tq, nh=nh)
    return pl.pallas_call(
        kern,
        grid=(batch, HA // nh, nq),
        in_specs=[
            pl.BlockSpec((nh, tq, LANES), lambda b, h, i: (h, b * nq + i, 0)),
            pl.BlockSpec((nh, seq, LANES), lambda b, h, i: (h, b, 0), pipeline_mode=pl.Buffered(1)),
            pl.BlockSpec((nh, seq, 2 * LANES), lambda b, h, i: (h, b, 0), pipeline_mode=pl.Buffered(1)),
            pl.BlockSpec((nh, N_META, LANES), lambda b, h, i: (h, 0, 0)),
            pl.BlockSpec((nh, N_META, 2 * LANES), lambda b, h, i: (h, 0, 0)),
            pl.BlockSpec((4, DHA), lambda b, h, i: (0, 0)),
            pl.BlockSpec((1, DVA), lambda b, h, i: (0, 0)),
        ],
        out_specs=pl.BlockSpec((tq, nh * LANES), lambda b, h, i: (b * nq + i, h)),
        out_shape=jax.ShapeDtypeStruct((batch * seq, HA * DVA), BF16),
        scratch_shapes=[pltpu.VMEM((nh, 2 * tq, LANES), F32), pltpu.VMEM((nh, 2 * tq, 2 * LANES), F32),
                        pltpu.VMEM((nh, 2 * tq, tq), F32)],
        compiler_params=_cparams(("parallel", "parallel", "arbitrary")),
        name="diff_prompt",
    )(qa, ka, va, mka, mva, lams, g_subln.reshape(1, DVA))


def _sb_live(carry_ref):
    return (jnp.min(carry_ref[...]) < SB_DEAD).astype(jnp.int32)


def _sb_prompt_kernel(q_ref, k_ref, v_ref, mk_ref, mv_ref, o_ref, carry_ref, acc_ref, *, tq, npair):
    i = pl.program_id(2)
    seg = SB_SEG
    per_q = tq // seg
    q2 = [_stack_masked(q_ref[p], DHB) for p in range(npair)]
    tri = _tri_incl(seg)
    carry_ref[...] = jnp.zeros(carry_ref.shape, F32)
    acc_ref[...] = jnp.zeros(acc_ref.shape, F32)

    row = lax.broadcasted_iota(jnp.int32, (tq, tq), 0)
    col = lax.broadcasted_iota(jnp.int32, (tq, tq), 1)
    strict = col < row
    mask2 = jnp.concatenate([strict, strict], axis=0)
    start = pl.multiple_of(i * tq, tq)
    for p in range(npair):
        _sb_update(q2[p], k_ref[p, pl.ds(start, tq), :], v_ref[p, pl.ds(start, tq), :], tri,
                   carry_ref.at[p], acc_ref.at[p], mask2)

    n_full = i * per_q

    def cond(state):
        t, live = state
        return jnp.logical_and(t < n_full, live > 0)

    def body(state):
        t, _ = state
        start = pl.multiple_of((n_full - 1 - t) * seg, seg)
        for p in range(npair):
            _sb_update(q2[p], k_ref[p, pl.ds(start, seg), :], v_ref[p, pl.ds(start, seg), :], tri,
                       carry_ref.at[p], acc_ref.at[p], None)
        return t + 1, _sb_live(carry_ref)

    lax.while_loop(cond, body, (jnp.int32(0), _sb_live(carry_ref)))

    @pl.when(_sb_live(carry_ref) > 0)
    def _():
        for p in range(npair):
            _sb_update(q2[p], mk_ref[p], mv_ref[p], tri[:N_META, :N_META], carry_ref.at[p], acc_ref.at[p], None)

    for p in range(npair):
        o_ref[:, p * LANES:(p + 1) * LANES] = _sb_finish(acc_ref.at[p], tq).astype(o_ref.dtype)


def _sb_prompt(qb, kb, vb, mkb, mvb, batch, seq, tq, npair=2):
    nq = seq // tq
    kern = functools.partial(_sb_prompt_kernel, tq=tq, npair=npair)
    return pl.pallas_call(
        kern,
        grid=(batch, HB // 2 // npair, nq),
        in_specs=[
            pl.BlockSpec((npair, tq, LANES), lambda b, h, i: (h, b * nq + i, 0)),
            pl.BlockSpec((npair, seq, LANES), lambda b, h, i: (h, b, 0)),
            pl.BlockSpec((npair, seq, LANES), lambda b, h, i: (h, b, 0)),
            pl.BlockSpec((npair, N_META, LANES), lambda b, h, i: (h, 0, 0)),
            pl.BlockSpec((npair, N_META, LANES), lambda b, h, i: (h, 0, 0)),
        ],
        out_specs=pl.BlockSpec((tq, npair * LANES), lambda b, h, i: (b * nq + i, h)),
        out_shape=jax.ShapeDtypeStruct((batch * seq, HB * DHB), BF16),
        scratch_shapes=[pltpu.VMEM((npair, 2 * tq, LANES), F32), pltpu.VMEM((npair, 2 * tq, LANES), F32)],
        compiler_params=_cparams(("parallel", "parallel", "arbitrary")),
        name="sb_prompt",
    )(qb, kb, vb, mkb, mvb)


def _sample_kernel(qa_ref, qb_ref, nka_ref, nva_ref, nkb_ref, nvb_ref,
                   cka_ref, cva_ref, ckb_ref, cvb_ref,
                   mka_ref, mva_ref, mkb_ref, mvb_ref, lam_ref, g_ref,
                   oa_ref, ob_ref, m_ref, acca_ref, carry_ref, accb_ref, *, ns, past):
    seg = SB_SEG
    n_seg = past // seg
    row = lax.broadcasted_iota(jnp.int32, (ns, ns), 0)
    col = lax.broadcasted_iota(jnp.int32, (ns, ns), 1)

    q2 = _stack_masked(qa_ref[0], DHA)
    m_ref[...] = jnp.full(m_ref.shape, NEG_BIG, F32)
    acca_ref[...] = jnp.zeros(acca_ref.shape, F32)
    _diff_update(q2, mka_ref[0], mva_ref[0], m_ref, acca_ref, None)
    cache_v = jnp.concatenate([cva_ref[0].astype(BF16), jnp.ones((past, LANES), BF16)], axis=1)
    _diff_update(q2, cka_ref[0].astype(BF16), cache_v, m_ref, acca_ref, None, k_is_t=True)
    mask = ((past + col) // CHUNK) <= ((past + row) // CHUNK)
    _diff_update(q2, nka_ref[0], nva_ref[0], m_ref, acca_ref, jnp.concatenate([mask, mask], axis=0))
    oa_ref[...] = _diff_finish(acca_ref, lam_ref, g_ref, ns).astype(oa_ref.dtype)

    q2 = _stack_masked(qb_ref[0], DHB)
    tri = _tri_incl(seg)
    carry_ref[...] = jnp.zeros(carry_ref.shape, F32)
    accb_ref[...] = jnp.zeros(accb_ref.shape, F32)
    strict = col < row
    _sb_update(q2, nkb_ref[0], nvb_ref[0], tri[:ns, :ns], carry_ref, accb_ref,
               jnp.concatenate([strict, strict], axis=0))
    for s in reversed(range(n_seg)):
        @pl.when(_sb_live(carry_ref) > 0)
        def _(s=s):
            k = ckb_ref[0, :, s * seg:(s + 1) * seg].astype(BF16)
            v = cvb_ref[0, :, s * seg:(s + 1) * seg].astype(BF16)
            _sb_update(q2, k, v, tri, carry_ref, accb_ref, None, kv_is_t=True)

    @pl.when(_sb_live(carry_ref) > 0)
    def _():
        _sb_update(q2, mkb_ref[0], mvb_ref[0], tri[:N_META, :N_META], carry_ref, accb_ref, None)

    ob_ref[...] = _sb_finish(accb_ref, ns).astype(ob_ref.dtype)


def _sample_attention(qa, qb, ka, va, kb, vb, cka, cva, ckb, cvb, mka, mva, mkb, mvb, lams, g_subln,
                      row0, nb, ns, past):
    blk0 = row0 // ns
    new = lambda w: pl.BlockSpec((1, ns, w), lambda b, h: (h, blk0 + b, 0))
    cache = pl.BlockSpec((1, past, LANES), lambda b, h: (b, 0, h))
    cache_t = pl.BlockSpec((1, LANES, past), lambda b, h: (b, h, 0))
    meta = lambda w: pl.BlockSpec((1, N_META, w), lambda b, h: (h, 0, 0))
    out = pl.BlockSpec((ns, LANES), lambda b, h: (b, h))
    kern = functools.partial(_sample_kernel, ns=ns, past=past)
    return pl.pallas_call(
        kern,
        grid=(nb, 4),
        in_specs=[new(LANES), new(LANES), new(LANES), new(2 * LANES), new(LANES), new(LANES),
                  cache_t, cache, cache_t, cache_t,
                  meta(LANES), meta(2 * LANES), meta(LANES), meta(LANES),
                  pl.BlockSpec((4, DHA), lambda b, h: (0, 0)),
                  pl.BlockSpec((1, DVA), lambda b, h: (0, 0))],
        out_specs=[out, out],
        out_shape=[jax.ShapeDtypeStruct((nb * ns, 512), BF16)] * 2,
        scratch_shapes=[pltpu.VMEM((2 * ns, LANES), F32), pltpu.VMEM((2 * ns, 2 * LANES), F32),
                        pltpu.VMEM((2 * ns, LANES), F32), pltpu.VMEM((2 * ns, LANES), F32)],
        compiler_params=_cparams(("parallel", "parallel")),
        name="sample_attn",
    )(qa, qb, ka, va, kb, vb, cka, cva, ckb, cvb, mka, mva, mkb, mvb, lams, g_subln.reshape(1, DVA))


def _merge_kernel(x_ref, oa_ref, ob_ref, gmix_ref, wg_ref, bg_ref, wa_ref, wb_ref, wo_ref,
                  gffn_ref, wrt_ref, br_ref, x2_ref, h2_ref, gates_ref):
    x = x_ref[...]
    d = x.shape[1]
    ms = jnp.mean(x * x, axis=-1, keepdims=True)
    h = (x * lax.rsqrt(ms + EPS) * gmix_ref[...]).astype(BF16)
    gate = jax.nn.sigmoid(_dot(h, wg_ref[...]) + bg_ref[...])
    merged = gate[:, :d] * _dot(oa_ref[...], wa_ref[...]) + gate[:, d:] * _dot(ob_ref[...], wb_ref[...])
    x2 = x + _dot(merged.astype(BF16), wo_ref[...])
    x2_ref[...] = x2
    ms2 = jnp.mean(x2 * x2, axis=-1, keepdims=True)
    h2 = x2 * lax.rsqrt(ms2 + EPS) * gffn_ref[...]
    h2_ref[...] = h2.astype(BF16)

    h_hi = h2.astype(BF16)
    h_lo = (h2 - h_hi.astype(F32)).astype(BF16)
    wr = wrt_ref[...]
    w_hi = wr.astype(BF16)
    w_lo = (wr - w_hi.astype(F32)).astype(BF16)
    logits = _nt_dot(w_hi, h_hi) + _nt_dot(w_hi, h_lo) + _nt_dot(w_lo, h_hi) + br_ref[...]

    sub = lax.broadcasted_iota(jnp.int32, logits.shape, 0).astype(F32)
    work = logits
    picks = []
    for _ in range(TOP_K):
        top = jnp.max(work, axis=0, keepdims=True)
        first = jnp.min(jnp.where(work == top, sub, float(N_EXPERTS)), axis=0, keepdims=True)
        hit = sub == first
        picks.append((top, hit))
        work = jnp.where(hit, -jnp.inf, work)
    exps = [jnp.exp(v - picks[0][0]) for v, _ in picks]
    denom = exps[0] + exps[1] + exps[2] + exps[3]
    gates = jnp.zeros(logits.shape, F32)
    for e, (_, hit) in zip(exps, picks):
        gates = gates + jnp.where(hit, e / denom, 0.0)
    gates_ref[...] = gates


def _merge(x, oa, ob, g_mix, w_gate, b_gate, w_a, w_b, w_out, g_ffn, w_router, b_router, tile):
    rows, d = x.shape
    row = lambda w: pl.BlockSpec((tile, w), lambda i: (i, 0))
    const = lambda shape: pl.BlockSpec(shape, lambda i: (0,) * len(shape))
    return pl.pallas_call(
        _merge_kernel,
        grid=(rows // tile,),
        in_specs=[row(d), row(512), row(512), const((1, d)), const((d, 2 * d)), const((1, 2 * d)),
                  const((512, d)), const((512, d)), const((d, d)), const((1, d)),
                  const((N_EXPERTS, d)), const((N_EXPERTS, 1))],
        out_specs=[row(d), row(d), pl.BlockSpec((N_EXPERTS, tile), lambda i: (0, i))],
        out_shape=[jax.ShapeDtypeStruct((rows, d), F32), jax.ShapeDtypeStruct((rows, d), BF16),
                   jax.ShapeDtypeStruct((N_EXPERTS, rows), F32)],
        compiler_params=_cparams(("parallel",)),
        name="merge",
    )(x, oa, ob, g_mix.reshape(1, d), w_gate, b_gate.reshape(1, 2 * d), w_a, w_b, w_out, g_ffn.reshape(1, d),
      w_router.T, b_router.reshape(N_EXPERTS, 1))


def _deinterleave_kernel(w_ref, glu_ref, lin_ref):
    r = lax.broadcasted_iota(jnp.int32, (2 * LANES, 2 * LANES), 0)
    c = lax.broadcasted_iota(jnp.int32, (2 * LANES, 2 * LANES), 1)
    src = jnp.where(c < LANES, 2 * c, 2 * (c - LANES) + 1)
    sel = jnp.where(r == src, 1.0, 0.0).astype(BF16)
    for j in range(w_ref.shape[2] // (2 * LANES)):
        chunk = w_ref[0, :, j * 2 * LANES:(j + 1) * 2 * LANES].astype(BF16)
        both = _dot(chunk, sel)
        glu_ref[0, :, j * LANES:(j + 1) * LANES] = both[:, :LANES].astype(BF16)
        lin_ref[0, :, j * LANES:(j + 1) * LANES] = both[:, LANES:].astype(BF16)


def _deinterleave(w_up):
    ne, d, two_f = w_up.shape
    cols = two_f
    out = jax.ShapeDtypeStruct((ne, d, two_f // 2), BF16)
    return pl.pallas_call(
        _deinterleave_kernel,
        grid=(ne, two_f // cols),
        in_specs=[pl.BlockSpec((1, d, cols), lambda e, j: (e, 0, j))],
        out_specs=[pl.BlockSpec((1, d, cols // 2), lambda e, j: (e, 0, j))] * 2,
        out_shape=[out, out],
        compiler_params=_cparams(("parallel", "parallel")),
        name="deinterleave",
    )(w_up)


def _to_bf16_kernel(w_ref, o_ref):
    o_ref[...] = w_ref[...].astype(BF16)


def _to_bf16(w):
    ne, a, b = w.shape
    spec = pl.BlockSpec((1, a, b), lambda e: (e, 0, 0))
    return pl.pallas_call(
        _to_bf16_kernel, grid=(ne,), in_specs=[spec], out_specs=spec,
        out_shape=jax.ShapeDtypeStruct(w.shape, BF16), compiler_params=_cparams(("parallel",)), name="to_bf16",
    )(w)


def _moe_kernel(h_ref, gt_ref, x2_ref, tri_ref, wg_ref, wl_ref, bg_ref, bl_ref, wd_ref, bd_ref, gf_ref,
                y_ref, rank_ref, *, nb):
    e = pl.program_id(1)
    sub = tri_ref.shape[0]
    ch = MOE_CHUNK
    subs = [slice(s * sub, (s + 1) * sub) for s in range(nb)]

    @pl.when(e == 0)
    def _():
        y_ref[...] = jnp.zeros(y_ref.shape, F32)
        member = jnp.where(gt_ref[...] > 0.0, 1.0, 0.0).astype(BF16)
        for sl in subs:
            rank_ref[:, sl] = _dot(member[:, sl], tri_ref[...])

    gate_e = gt_ref[pl.ds(e, 1), :]
    routed = gate_e > 0.0
    key = jnp.where(routed, rank_ref[pl.ds(e, 1), :], -1.0)
    hits = jnp.where(routed, 1.0, 0.0)
    n = jnp.sum(hits[:, subs[0]]).astype(jnp.int32)
    for sl in subs[1:]:
        n = jnp.maximum(n, jnp.sum(hits[:, sl]).astype(jnp.int32))

    def body(c, carry):
        slot = (lax.broadcasted_iota(jnp.int32, (ch, sub), 0) + c * ch).astype(F32)
        sels = [key[:, sl] == slot for sl in subs]
        gs = [jnp.where(sel, 1.0, 0.0).astype(BF16) for sel in sels]
        xg = jnp.concatenate([_dot(g, h_ref[sl, :]) for g, sl in zip(gs, subs)], axis=0).astype(BF16)
        w = jnp.concatenate([jnp.sum(jnp.where(sel, gate_e[:, sl], 0.0), axis=1, keepdims=True)
                             for sel, sl in zip(sels, subs)], axis=0)
        glu = jnp.minimum(_dot(xg, wg_ref[0]) + bg_ref[0], SWIGLU_LIMIT)
        lin = jnp.clip(_dot(xg, wl_ref[0]) + bl_ref[0], -SWIGLU_LIMIT, SWIGLU_LIMIT)
        act = glu * jax.nn.sigmoid(SWIGLU_ALPHA * glu) * (lin + 1.0)
        ys = ((_dot(act.astype(BF16), wd_ref[0]) + bd_ref[0]) * w).astype(BF16)
        for s, (g, sl) in enumerate(zip(gs, subs)):
            y_ref[sl, :] += lax.dot_general(g, ys[s * ch:(s + 1) * ch], (((0,), (0,)), ((), ())),
                                            preferred_element_type=F32)
        return carry

    lax.fori_loop(0, (n + ch - 1) // ch, body, 0)

    @pl.when(e == pl.num_programs(1) - 1)
    def _():
        x3 = x2_ref[...] + y_ref[...]
        ms = jnp.mean(x3 * x3, axis=-1, keepdims=True)
        y_ref[...] = x3 * lax.rsqrt(ms + EPS) * gf_ref[...]


def _moe(h2, gates_t, x2, w_glu, w_lin, b_glu, b_lin, w_down, b_down, g_final, sub, nb):
    rows, d = x2.shape
    dff = w_glu.shape[2]
    tile = sub * nb
    once = dict(pipeline_mode=pl.Buffered(1))
    row = lambda w, **kw: pl.BlockSpec((tile, w), lambda i, e: (i, 0), **kw)
    exp = lambda a, b: pl.BlockSpec((1, a, b), lambda i, e: (e, 0, 0))
    r = lax.broadcasted_iota(jnp.int32, (sub, sub), 0)
    c = lax.broadcasted_iota(jnp.int32, (sub, sub), 1)
    tri = jnp.where(r < c, 1.0, 0.0).astype(BF16)
    return pl.pallas_call(
        functools.partial(_moe_kernel, nb=nb),
        grid=(rows // tile, N_EXPERTS),
        in_specs=[row(d, **once), pl.BlockSpec((N_EXPERTS, tile), lambda i, e: (0, i), **once), row(d, **once),
                  pl.BlockSpec((sub, sub), lambda i, e: (0, 0), **once),
                  exp(d, dff), exp(d, dff), exp(1, dff), exp(1, dff),
                  exp(dff, d), exp(1, d), pl.BlockSpec((1, d), lambda i, e: (0, 0))],
        out_specs=row(d),
        out_shape=jax.ShapeDtypeStruct((rows, d), F32),
        scratch_shapes=[pltpu.VMEM((N_EXPERTS, tile), F32)],
        compiler_params=_cparams(("parallel", "arbitrary")),
        name="moe",
    )(h2, gates_t, x2, tri, w_glu, w_lin, b_glu, b_lin, w_down, b_down, g_final.reshape(1, d))


def _pick(n, pref):
    t = min(n, pref)
    assert n % t == 0, (n, pref)
    return t


def kernel(x_prompt, x_sample, cache_diff_k, cache_diff_v, cache_sb_k, cache_sb_v, meta_tokens, g_mix, w_in,
           lambda_q1, lambda_k1, lambda_q2, lambda_k2, g_subln, w_branch_a, w_branch_b, w_gate, b_gate, w_out,
           g_ffn, w_router, b_router, w_up, b_up, w_down, b_down, g_final):
    b, n, d = x_prompt.shape
    bs, ns, _ = x_sample.shape
    past = cache_diff_k.shape[1]
    rows_p, rows_s = b * n, bs * ns
    assert n % SB_SEG == 0 and past % SB_SEG == 0

    w_in_bf = w_in.astype(BF16)
    lams = jnp.stack([lambda_q1, lambda_k1, lambda_q2, lambda_k2]).astype(F32)

    tile_p = _pick(n, 512)
    per_seq = n // tile_p
    p32 = _proj(x_prompt.reshape(rows_p, d), g_mix, w_in_bf, _rotary_tables(N_META + jnp.arange(n)),
                tile_p, lambda i: i % per_seq)
    tile_s = _pick(rows_s, 512)
    assert tile_s % ns == 0
    s32 = _proj(x_sample.reshape(rows_s, d), g_mix, w_in_bf,
                _rotary_tables(N_META + past + (jnp.arange(tile_s) % ns)), tile_s, lambda i: 0)
    m32 = _proj(meta_tokens, g_mix, w_in_bf, _rotary_tables(jnp.arange(N_META)), N_META, lambda i: i)
    _, _, _, _, qa, qb, ka, kb, va, vb = p32
    _, _, _, _, sqa, sqb, ska, skb, sva, svb = s32
    _, _, _, _, _, _, mka, mkb, mva, mvb = m32

    tq = _pick(n, 512)
    oa_p = _diff_prompt(qa, ka, va, mka, mva, lams, g_subln, b, n, tq)
    ob_p = _sb_prompt(qb, kb, vb, mkb, mvb, b, n, tq)
    oa_s, ob_s = _sample_attention(
        sqa, sqb, ska, sva, skb, svb,
        jnp.transpose(cache_diff_k, (0, 2, 3, 4, 1)).reshape(bs, 512, past), cache_diff_v.reshape(bs, past, 512),
        jnp.transpose(cache_sb_k, (0, 2, 3, 1)).reshape(bs, 512, past),
        jnp.transpose(cache_sb_v, (0, 2, 3, 1)).reshape(bs, 512, past),
        mka, mva, mkb, mvb, lams, g_subln, 0, bs, ns, past)

    dff = w_down.shape[1]
    w_glu, w_lin = _deinterleave(w_up)
    w_down_bf = _to_bf16(w_down)
    b_glu = b_up[:, 0::2].reshape(N_EXPERTS, 1, dff)
    b_lin = b_up[:, 1::2].reshape(N_EXPERTS, 1, dff)
    merge_w = (g_mix, w_gate.astype(BF16), b_gate, w_branch_a.astype(BF16), w_branch_b.astype(BF16),
               w_out.astype(BF16), g_ffn, w_router, b_router)

    def ffn(x_rows, oa, ob, tile):
        x2, h2, gates_t = _merge(x_rows, oa, ob, *merge_w, tile)
        sub = _pick(x_rows.shape[0], MOE_SUB)
        nb = 2 if x_rows.shape[0] % (2 * sub) == 0 else 1
        return _moe(h2, gates_t, x2, w_glu, w_lin, b_glu, b_lin, w_down_bf, b_down.reshape(N_EXPERTS, 1, d),
                    g_final, sub, nb)

    y_p = ffn(x_prompt.reshape(rows_p, d), oa_p, ob_p, tile_p)
    y_s = ffn(x_sample.reshape(rows_s, d), oa_s, ob_s, tile_s)

    def with_meta(meta_rows, rows32, shape):
        full = jnp.concatenate([jnp.broadcast_to(meta_rows[None], (b, N_META, 512)), rows32.reshape(b, n, 512)], axis=1)
        return full.reshape((b, N_META + n) + shape)

    shapes = ((HA, 2, DHA), (HB, DHB), (HA, DVA), (HB, DHB))
    kv_p = [with_meta(m32[j], p32[j], shapes[j]) for j in range(4)]
    kv_s = [s32[j].reshape((bs, ns) + shapes[j]) for j in range(4)]
    return (y_p.reshape(b, n, d), y_s.reshape(bs, ns, d), kv_p[0], kv_p[2], kv_p[1], kv_p[3],
            kv_s[0], kv_s[2], kv_s[1], kv_s[3])
```

```python
import functools
import math

import jax
import jax.numpy as jnp
from jax import lax
from jax.experimental import pallas as pl
from jax.experimental.pallas import tpu as pltpu

F32 = jnp.float32
BF16 = jnp.bfloat16

N_META = 16
CHUNK = 64
HA = 4
DHA = 64
DVA = 128
HB = 8
DHB = 64
ROT_DIM = 16
ROPE_THETA = 500000.0
N_EXPERTS = 32
TOP_K = 4
SWIGLU_ALPHA = 1.702
SWIGLU_LIMIT = 7.0
EPS = 1e-5
LAM_INIT = 0.8 - 0.6 * math.exp(-0.3 * 0)
LOG2E = 1.4426950408889634

LANES = 128
NEG_BIG = -1e30
SB_SEG = 256
SB_DEAD = 150.0
VMEM_LIMIT = 56 * 1024 * 1024
MOE_SUB = 1024
MOE_CHUNK = 160
MOE_TAIL = 64


def _cparams(sem):
    return pltpu.CompilerParams(dimension_semantics=sem, vmem_limit_bytes=VMEM_LIMIT)


def _nt_dot(a, b):
    return lax.dot_general(a, b, (((1,), (1,)), ((), ())), preferred_element_type=F32)


def _dot(a, b):
    return jnp.dot(a, b, preferred_element_type=F32)


def _proj_kernel(x_ref, g_ref, w_ref, cos_ref, sa_ref, sb_ref,
                 ka32_ref, kb32_ref, va32_ref, vb32_ref,
                 qa_ref, qb_ref, ka_ref, kb_ref, va_ref, vb_ref):
    x = x_ref[...]
    ms = jnp.mean(x * x, axis=-1, keepdims=True)
    h = (x * lax.rsqrt(ms + EPS) * g_ref[...]).astype(BF16)
    proj = _dot(h, w_ref[...])
    cos = cos_ref[...]
    sa = sa_ref[...]
    sb = sb_ref[...]
    ones = jnp.ones((x.shape[0], LANES), BF16)

    def rot(c):
        return c * cos + pltpu.roll(c, 8, 1) * sa + pltpu.roll(c, LANES - 8, 1) * sb

    for j in range(4):
        lo = j * LANES
        qa = rot(proj[:, lo:lo + LANES]) * (DHA ** -0.5 * LOG2E)
        qa_ref[j] = qa.astype(BF16)
        qb = proj[:, 512 + lo:512 + lo + LANES] * (DHB ** -0.5 * LOG2E)
        qb_ref[j] = qb.astype(BF16)
        ka = rot(proj[:, 1024 + lo:1024 + lo + LANES])
        ka32_ref[:, lo:lo + LANES] = ka
        ka_ref[j] = ka.astype(BF16)
        kb = proj[:, 1536 + lo:1536 + lo + LANES]
        kb32_ref[:, lo:lo + LANES] = kb
        kb_ref[j] = kb.astype(BF16)
        va = proj[:, 2048 + lo:2048 + lo + LANES]
        va32_ref[:, lo:lo + LANES] = va
        va_ref[j, :, :LANES] = va.astype(BF16)
        va_ref[j, :, LANES:] = ones
        vb = proj[:, 2560 + lo:2560 + lo + LANES]
        vb32_ref[:, lo:lo + LANES] = vb
        vb_ref[j] = vb.astype(BF16)


def _proj(x, g_mix, w_in_bf, tabs, tile, tab_index):
    rows, d = x.shape
    n_tiles = rows // tile
    row_spec = lambda w: pl.BlockSpec((tile, w), lambda i: (i, 0))
    head_spec = lambda w: pl.BlockSpec((4, tile, w), lambda i: (0, i, 0))
    tab_spec = pl.BlockSpec((tile, LANES), lambda i: (tab_index(i), 0))
    const = lambda shape: pl.BlockSpec(shape, lambda i: (0,) * len(shape))
    f32_rows = jax.ShapeDtypeStruct((rows, 512), F32)
    bf_heads = jax.ShapeDtypeStruct((4, rows, LANES), BF16)
    return pl.pallas_call(
        _proj_kernel,
        grid=(n_tiles,),
        in_specs=[row_spec(d), const((1, d)), const(w_in_bf.shape), tab_spec, tab_spec, tab_spec],
        out_specs=[row_spec(512)] * 4 + [head_spec(LANES)] * 4 + [head_spec(2 * LANES), head_spec(LANES)],
        out_shape=[f32_rows] * 4 + [bf_heads] * 4
        + [jax.ShapeDtypeStruct((4, rows, 2 * LANES), BF16), bf_heads],
        compiler_params=_cparams(("parallel",)),
        name="proj",
    )(x, g_mix.reshape(1, d), w_in_bf, *tabs)


def _rotary_tables(pos):
    half = ROT_DIM // 2
    inv_freq = ROPE_THETA ** (-jnp.arange(0, ROT_DIM, 2, dtype=F32) / ROT_DIM)
    ang = pos.astype(F32)[:, None] * inv_freq[None, :]
    cos, sin = jnp.cos(ang), jnp.sin(ang)
    lane = jnp.arange(LANES) % DHA
    idx = lane % half
    first = (lane < half)[None, :]
    second = ((lane >= half) & (lane < ROT_DIM))[None, :]
    cos_t = jnp.where(first | second, cos[:, idx], 1.0)
    sa_t = jnp.where(second, sin[:, idx], 0.0)
    sb_t = jnp.where(first, -sin[:, idx], 0.0)
    return cos_t, sa_t, sb_t


def _stack_masked(q, split):
    lane = lax.broadcasted_iota(jnp.int32, q.shape, 1)
    zero = jnp.zeros_like(q)
    return jnp.concatenate([jnp.where(lane < split, q, zero), jnp.where(lane >= split, q, zero)], axis=0)


def _diff_scores(q2, k, k_is_t=False):
    return _dot(q2, k) if k_is_t else _nt_dot(q2, k)


def _diff_accumulate(s, vaug, m_ref, acc_ref, mask):
    if mask is not None:
        s = jnp.where(mask, s, NEG_BIG)
    m_prev = m_ref[...]
    m_new = jnp.maximum(m_prev, jnp.max(s, axis=-1, keepdims=True))
    alpha = jnp.exp2(m_prev - m_new)
    p = jnp.exp2(s - m_new[:, :1])
    pv = _dot(p.astype(BF16), vaug)
    acc_ref[...] = jnp.concatenate([alpha, alpha], axis=1) * acc_ref[...] + pv
    m_ref[...] = m_new


def _diff_update(q2, k, vaug, m_ref, acc_ref, mask, k_is_t=False):
    _diff_accumulate(_diff_scores(q2, k, k_is_t), vaug, m_ref, acc_ref, mask)


def _diff_finish(acc_ref, lam_ref, g_ref, t):
    lq1, lk1, lq2, lk2 = (lam_ref[i:i + 1, :] for i in range(4))
    lam = (jnp.exp(jnp.sum(lq1 * lk1, axis=-1, keepdims=True))
           - jnp.exp(jnp.sum(lq2 * lk2, axis=-1, keepdims=True)) + LAM_INIT)
    acc = acc_ref[...]
    o0 = acc[:t, :LANES] / acc[:t, LANES:]
    o1 = acc[t:, :LANES] / acc[t:, LANES:]
    o = o0 - lam * o1
    ms = jnp.mean(o * o, axis=-1, keepdims=True)
    return o * lax.rsqrt(ms + EPS) * g_ref[...] * (1.0 - LAM_INIT)


def _tri_incl(n):
    r = lax.broadcasted_iota(jnp.int32, (n, n), 0)
    c = lax.broadcasted_iota(jnp.int32, (n, n), 1)
    return jnp.where(r >= c, 1.0, 0.0).astype(BF16)


def _sb_update(q2, k, v, tri, carry_ref, acc_ref, mask, kv_is_t=False):
    z = _dot(q2, k) if kv_is_t else _nt_dot(q2, k)
    nlk = jnp.maximum(z, 0.0) + jnp.log2(1.0 + jnp.exp2(-jnp.abs(z)))
    if mask is not None:
        nlk = jnp.where(mask, nlk, 0.0)
    hi = nlk.astype(BF16)
    lo = (nlk - hi.astype(F32)).astype(BF16)
    seg = tri.shape[0]
    c = carry_ref[...][:, :1]
    args = []
    for j in reversed(range(z.shape[1] // seg)):
        sl = slice(j * seg, (j + 1) * seg)
        incl = _dot(hi[:, sl], tri) + _dot(lo[:, sl], tri)
        args.append(z[:, sl] - incl - c)
        c = c + incl[:, :1]
    arg = args[0] if len(args) == 1 else jnp.concatenate(args[::-1], axis=1)
    a = jnp.exp2(arg)
    if mask is not None:
        a = jnp.where(mask, a, 0.0)
    a = a.astype(BF16)
    acc_ref[...] += _nt_dot(a, v) if kv_is_t else _dot(a, v)
    carry_ref[...] = jnp.broadcast_to(c, carry_ref.shape)


def _sb_finish(acc_ref, t):
    acc = acc_ref[...]
    lane = lax.broadcasted_iota(jnp.int32, (t, LANES), 1)
    return jnp.where(lane < DHB, acc[:t], acc[t:])


def _diff_prompt_kernel(q_ref, k_ref, v_ref, mk_ref, mv_ref, lam_ref, g_ref, o_ref,
                        m_ref, acc_ref, s_ref, *, t, nh):
    i = pl.program_id(2)
    q2 = [_stack_masked(q_ref[h], DHA) for h in range(nh)]
    m_ref[...] = jnp.full(m_ref.shape, NEG_BIG, F32)
    acc_ref[...] = jnp.zeros(acc_ref.shape, F32)
    for h in range(nh):
        s_ref[h] = _diff_scores(q2[h], k_ref[h, 0:t, :])
        _diff_update(q2[h], mk_ref[h], mv_ref[h], m_ref.at[h], acc_ref.at[h], None)

    def body(j, c):
        cur = pl.multiple_of(j * t, t)
        nxt = pl.multiple_of((j + 1) * t, t)
        for h in range(nh):
            _diff_accumulate(s_ref[h], v_ref[h, pl.ds(cur, t), :], m_ref.at[h], acc_ref.at[h], None)
            s_ref[h] = _diff_scores(q2[h], k_ref[h, pl.ds(nxt, t), :])
        return c

    lax.fori_loop(0, i, body, 0)

    row = lax.broadcasted_iota(jnp.int32, (t, t), 0)
    col = lax.broadcasted_iota(jnp.int32, (t, t), 1)
    mask = (col // CHUNK) <= (row // CHUNK)
    mask2 = jnp.concatenate([mask, mask], axis=0)
    diag = pl.multiple_of(i * t, t)
    for h in range(nh):
        _diff_accumulate(s_ref[h], v_ref[h, pl.ds(diag, t), :], m_ref.at[h], acc_ref.at[h], mask2)
        o_ref[:, h * LANES:(h + 1) * LANES] = _diff_finish(acc_ref.at[h], lam_ref, g_ref, t).astype(o_ref.dtype)


def _diff_prompt(qa, ka, va, mka, mva, lams, g_subln, batch, seq, tq, nh=4):
    nq = seq // tq
    kern = functools.partial(_diff_prompt_kernel, t=tq, nh=nh)
    return pl.pallas_call(
        kern,
        grid=(batch, HA // nh, nq),
        in_specs=[
            pl.BlockSpec((nh, tq, LANES), lambda b, h, i: (h, b * nq + i, 0)),
            pl.BlockSpec((nh, seq, LANES), lambda b, h, i: (h, b, 0), pipeline_mode=pl.Buffered(1)),
            pl.BlockSpec((nh, seq, 2 * LANES), lambda b, h, i: (h, b, 0), pipeline_mode=pl.Buffered(1)),
            pl.BlockSpec((nh, N_META, LANES), lambda b, h, i: (h, 0, 0)),
            pl.BlockSpec((nh, N_META, 2 * LANES), lambda b, h, i: (h, 0, 0)),
            pl.BlockSpec((4, DHA), lambda b, h, i: (0, 0)),
            pl.BlockSpec((1, DVA), lambda b, h, i: (0, 0)),
        ],
        out_specs=pl.BlockSpec((tq, nh * LANES), lambda b, h, i: (b * nq + i, h)),
        out_shape=jax.ShapeDtypeStruct((batch * seq, HA * DVA), BF16),
        scratch_shapes=[pltpu.VMEM((nh, 2 * tq, LANES), F32), pltpu.VMEM((nh, 2 * tq, 2 * LANES), F32),
                        pltpu.VMEM((nh, 2 * tq, tq), F32)],
        compiler_params=_cparams(("parallel", "parallel", "arbitrary")),
        name="diff_prompt",
    )(qa, ka, va, mka, mva, lams, g_subln.reshape(1, DVA))


def _sb_live(carry_ref):
    return (jnp.min(carry_ref[...]) < SB_DEAD).astype(jnp.int32)


def _sb_prompt_kernel(q_ref, k_ref, v_ref, mk_ref, mv_ref, o_ref, carry_ref, acc_ref, *, tq, npair):
    i = pl.program_id(2)
    seg = SB_SEG
    per_q = tq // seg
    q2 = [_stack_masked(q_ref[p], DHB) for p in range(npair)]
    tri = _tri_incl(seg)
    carry_ref[...] = jnp.zeros(carry_ref.shape, F32)
    acc_ref[...] = jnp.zeros(acc_ref.shape, F32)

    row = lax.broadcasted_iota(jnp.int32, (tq, tq), 0)
    col = lax.broadcasted_iota(jnp.int32, (tq, tq), 1)
    strict = col < row
    mask2 = jnp.concatenate([strict, strict], axis=0)
    start = pl.multiple_of(i * tq, tq)
    for p in range(npair):
        _sb_update(q2[p], k_ref[p, pl.ds(start, tq), :], v_ref[p, pl.ds(start, tq), :], tri,
                   carry_ref.at[p], acc_ref.at[p], mask2)

    n_full = i * per_q

    def cond(state):
        t, live = state
        return jnp.logical_and(t < n_full, live > 0)

    def body(state):
        t, _ = state
        start = pl.multiple_of((n_full - 1 - t) * seg, seg)
        for p in range(npair):
            _sb_update(q2[p], k_ref[p, pl.ds(start, seg), :], v_ref[p, pl.ds(start, seg), :], tri,
                       carry_ref.at[p], acc_ref.at[p], None)
        return t + 1, _sb_live(carry_ref)

    lax.while_loop(cond, body, (jnp.int32(0), _sb_live(carry_ref)))

    @pl.when(_sb_live(carry_ref) > 0)
    def _():
        for p in range(npair):
            _sb_update(q2[p], mk_ref[p], mv_ref[p], tri[:N_META, :N_META], carry_ref.at[p], acc_ref.at[p], None)

    for p in range(npair):
        o_ref[:, p * LANES:(p + 1) * LANES] = _sb_finish(acc_ref.at[p], tq).astype(o_ref.dtype)


def _sb_prompt(qb, kb, vb, mkb, mvb, batch, seq, tq, npair=2):
    nq = seq // tq
    kern = functools.partial(_sb_prompt_kernel, tq=tq, npair=npair)
    return pl.pallas_call(
        kern,
        grid=(batch, HB // 2 // npair, nq),
        in_specs=[
            pl.BlockSpec((npair, tq, LANES), lambda b, h, i: (h, b * nq + i, 0)),
            pl.BlockSpec((npair, seq, LANES), lambda b, h, i: (h, b, 0)),
            pl.BlockSpec((npair, seq, LANES), lambda b, h, i: (h, b, 0)),
            pl.BlockSpec((npair, N_META, LANES), lambda b, h, i: (h, 0, 0)),
            pl.BlockSpec((npair, N_META, LANES), lambda b, h, i: (h, 0, 0)),
        ],
        out_specs=pl.BlockSpec((tq, npair * LANES), lambda b, h, i: (b * nq + i, h)),
        out_shape=jax.ShapeDtypeStruct((batch * seq, HB * DHB), BF16),
        scratch_shapes=[pltpu.VMEM((npair, 2 * tq, LANES), F32), pltpu.VMEM((npair, 2 * tq, LANES), F32)],
        compiler_params=_cparams(("parallel", "parallel", "arbitrary")),
        name="sb_prompt",
    )(qb, kb, vb, mkb, mvb)


def _sample_kernel(qa_ref, qb_ref, nka_ref, nva_ref, nkb_ref, nvb_ref,
                   cka_ref, cva_ref, ckb_ref, cvb_ref,
                   mka_ref, mva_ref, mkb_ref, mvb_ref, lam_ref, g_ref,
                   oa_ref, ob_ref, m_ref, acca_ref, carry_ref, accb_ref, *, ns, past):
    seg = SB_SEG
    n_seg = past // seg
    row = lax.broadcasted_iota(jnp.int32, (ns, ns), 0)
    col = lax.broadcasted_iota(jnp.int32, (ns, ns), 1)

    q2 = _stack_masked(qa_ref[0], DHA)
    m_ref[...] = jnp.full(m_ref.shape, NEG_BIG, F32)
    acca_ref[...] = jnp.zeros(acca_ref.shape, F32)
    _diff_update(q2, mka_ref[0], mva_ref[0], m_ref, acca_ref, None)
    cache_v = jnp.concatenate([cva_ref[0].astype(BF16), jnp.ones((past, LANES), BF16)], axis=1)
    _diff_update(q2, cka_ref[0].astype(BF16), cache_v, m_ref, acca_ref, None, k_is_t=True)
    mask = ((past + col) // CHUNK) <= ((past + row) // CHUNK)
    _diff_update(q2, nka_ref[0], nva_ref[0], m_ref, acca_ref, jnp.concatenate([mask, mask], axis=0))
    oa_ref[...] = _diff_finish(acca_ref, lam_ref, g_ref, ns).astype(oa_ref.dtype)

    q2 = _stack_masked(qb_ref[0], DHB)
    tri = _tri_incl(seg)
    carry_ref[...] = jnp.zeros(carry_ref.shape, F32)
    accb_ref[...] = jnp.zeros(accb_ref.shape, F32)
    strict = col < row
    _sb_update(q2, nkb_ref[0], nvb_ref[0], tri[:ns, :ns], carry_ref, accb_ref,
               jnp.concatenate([strict, strict], axis=0))
    for s in reversed(range(n_seg)):
        @pl.when(_sb_live(carry_ref) > 0)
        def _(s=s):
            k = ckb_ref[0, :, s * seg:(s + 1) * seg].astype(BF16)
            v = cvb_ref[0, :, s * seg:(s + 1) * seg].astype(BF16)
            _sb_update(q2, k, v, tri, carry_ref, accb_ref, None, kv_is_t=True)

    @pl.when(_sb_live(carry_ref) > 0)
    def _():
        _sb_update(q2, mkb_ref[0], mvb_ref[0], tri[:N_META, :N_META], carry_ref, accb_ref, None)

    ob_ref[...] = _sb_finish(accb_ref, ns).astype(ob_ref.dtype)


def _sample_attention(qa, qb, ka, va, kb, vb, cka, cva, ckb, cvb, mka, mva, mkb, mvb, lams, g_subln,
                      row0, nb, ns, past):
    blk0 = row0 // ns
    new = lambda w: pl.BlockSpec((1, ns, w), lambda b, h: (h, blk0 + b, 0))
    cache = pl.BlockSpec((1, past, LANES), lambda b, h: (b, 0, h))
    cache_t = pl.BlockSpec((1, LANES, past), lambda b, h: (b, h, 0))
    meta = lambda w: pl.BlockSpec((1, N_META, w), lambda b, h: (h, 0, 0))
    out = pl.BlockSpec((ns, LANES), lambda b, h: (b, h))
    kern = functools.partial(_sample_kernel, ns=ns, past=past)
    return pl.pallas_call(
        kern,
        grid=(nb, 4),
        in_specs=[new(LANES), new(LANES), new(LANES), new(2 * LANES), new(LANES), new(LANES),
                  cache_t, cache, cache_t, cache_t,
                  meta(LANES), meta(2 * LANES), meta(LANES), meta(LANES),
                  pl.BlockSpec((4, DHA), lambda b, h: (0, 0)),
                  pl.BlockSpec((1, DVA), lambda b, h: (0, 0))],
        out_specs=[out, out],
        out_shape=[jax.ShapeDtypeStruct((nb * ns, 512), BF16)] * 2,
        scratch_shapes=[pltpu.VMEM((2 * ns, LANES), F32), pltpu.VMEM((2 * ns, 2 * LANES), F32),
                        pltpu.VMEM((2 * ns, LANES), F32), pltpu.VMEM((2 * ns, LANES), F32)],
        compiler_params=_cparams(("parallel", "parallel")),
        name="sample_attn",
    )(qa, qb, ka, va, kb, vb, cka, cva, ckb, cvb, mka, mva, mkb, mvb, lams, g_subln.reshape(1, DVA))


def _merge_kernel(x_ref, oa_ref, ob_ref, gmix_ref, wg_ref, bg_ref, wa_ref, wb_ref, wo_ref,
                  gffn_ref, wrt_ref, br_ref, x2_ref, h2_ref, gates_ref):
    x = x_ref[...]
    d = x.shape[1]
    ms = jnp.mean(x * x, axis=-1, keepdims=True)
    h = (x * lax.rsqrt(ms + EPS) * gmix_ref[...]).astype(BF16)
    gate = jax.nn.sigmoid(_dot(h, wg_ref[...]) + bg_ref[...])
    merged = gate[:, :d] * _dot(oa_ref[...], wa_ref[...]) + gate[:, d:] * _dot(ob_ref[...], wb_ref[...])
    x2 = x + _dot(merged.astype(BF16), wo_ref[...])
    x2_ref[...] = x2
    ms2 = jnp.mean(x2 * x2, axis=-1, keepdims=True)
    h2 = x2 * lax.rsqrt(ms2 + EPS) * gffn_ref[...]
    h2_ref[...] = h2.astype(BF16)

    h_hi = h2.astype(BF16)
    h_lo = (h2 - h_hi.astype(F32)).astype(BF16)
    wr = wrt_ref[...]
    w_hi = wr.astype(BF16)
    w_lo = (wr - w_hi.astype(F32)).astype(BF16)
    logits = _nt_dot(w_hi, h_hi) + _nt_dot(w_hi, h_lo) + _nt_dot(w_lo, h_hi) + br_ref[...]

    sub = lax.broadcasted_iota(jnp.int32, logits.shape, 0).astype(F32)
    work = logits
    picks = []
    for _ in range(TOP_K):
        top = jnp.max(work, axis=0, keepdims=True)
        first = jnp.min(jnp.where(work == top, sub, float(N_EXPERTS)), axis=0, keepdims=True)
        hit = sub == first
        picks.append((top, hit))
        work = jnp.where(hit, -jnp.inf, work)
    exps = [jnp.exp(v - picks[0][0]) for v, _ in picks]
    denom = exps[0] + exps[1] + exps[2] + exps[3]
    gates = jnp.zeros(logits.shape, F32)
    for e, (_, hit) in zip(exps, picks):
        gates = gates + jnp.where(hit, e / denom, 0.0)
    gates_ref[...] = gates


def _merge(x, oa, ob, g_mix, w_gate, b_gate, w_a, w_b, w_out, g_ffn, w_router, b_router, tile):
    rows, d = x.shape
    row = lambda w: pl.BlockSpec((tile, w), lambda i: (i, 0))
    const = lambda shape: pl.BlockSpec(shape, lambda i: (0,) * len(shape))
    return pl.pallas_call(
        _merge_kernel,
        grid=(rows // tile,),
        in_specs=[row(d), row(512), row(512), const((1, d)), const((d, 2 * d)), const((1, 2 * d)),
                  const((512, d)), const((512, d)), const((d, d)), const((1, d)),
                  const((N_EXPERTS, d)), const((N_EXPERTS, 1))],
        out_specs=[row(d), row(d), pl.BlockSpec((N_EXPERTS, tile), lambda i: (0, i))],
        out_shape=[jax.ShapeDtypeStruct((rows, d), F32), jax.ShapeDtypeStruct((rows, d), BF16),
                   jax.ShapeDtypeStruct((N_EXPERTS, rows), F32)],
        compiler_params=_cparams(("parallel",)),
        name="merge",
    )(x, oa, ob, g_mix.reshape(1, d), w_gate, b_gate.reshape(1, 2 * d), w_a, w_b, w_out, g_ffn.reshape(1, d),
      w_router.T, b_router.reshape(N_EXPERTS, 1))


def _deinterleave_kernel(w_ref, glu_ref, lin_ref):
    r = lax.broadcasted_iota(jnp.int32, (2 * LANES, 2 * LANES), 0)
    c = lax.broadcasted_iota(jnp.int32, (2 * LANES, 2 * LANES), 1)
    src = jnp.where(c < LANES, 2 * c, 2 * (c - LANES) + 1)
    sel = jnp.where(r == src, 1.0, 0.0).astype(BF16)
    for j in range(w_ref.shape[2] // (2 * LANES)):
        chunk = w_ref[0, :, j * 2 * LANES:(j + 1) * 2 * LANES].astype(BF16)
        both = _dot(chunk, sel)
        glu_ref[0, :, j * LANES:(j + 1) * LANES] = both[:, :LANES].astype(BF16)
        lin_ref[0, :, j * LANES:(j + 1) * LANES] = both[:, LANES:].astype(BF16)


def _deinterleave(w_up):
    ne, d, two_f = w_up.shape
    cols = two_f
    out = jax.ShapeDtypeStruct((ne, d, two_f // 2), BF16)
    return pl.pallas_call(
        _deinterleave_kernel,
        grid=(ne, two_f // cols),
        in_specs=[pl.BlockSpec((1, d, cols), lambda e, j: (e, 0, j))],
        out_specs=[pl.BlockSpec((1, d, cols // 2), lambda e, j: (e, 0, j))] * 2,
        out_shape=[out, out],
        compiler_params=_cparams(("parallel", "parallel")),
        name="deinterleave",
    )(w_up)


def _to_bf16_kernel(w_ref, o_ref):
    o_ref[...] = w_ref[...].astype(BF16)


def _to_bf16(w):
    ne, a, b = w.shape
    spec = pl.BlockSpec((1, a, b), lambda e: (e, 0, 0))
    return pl.pallas_call(
        _to_bf16_kernel, grid=(ne,), in_specs=[spec], out_specs=spec,
        out_shape=jax.ShapeDtypeStruct(w.shape, BF16), compiler_params=_cparams(("parallel",)), name="to_bf16",
    )(w)


def _moe_kernel(h_ref, gt_ref, x2_ref, tri_ref, wg_ref, wl_ref, bg_ref, bl_ref, wd_ref, bd_ref, gf_ref,
                y_ref, rank_ref, *, nb):
    e = pl.program_id(1)
    sub = tri_ref.shape[0]
    subs = [slice(s * sub, (s + 1) * sub) for s in range(nb)]

    @pl.when(e == 0)
    def _():
        y_ref[...] = jnp.zeros(y_ref.shape, F32)
        member = jnp.where(gt_ref[...] > 0.0, 1.0, 0.0).astype(BF16)
        for sl in subs:
            rank_ref[:, sl] = _dot(member[:, sl], tri_ref[...])

    gate_e = gt_ref[pl.ds(e, 1), :]
    routed = gate_e > 0.0
    key = jnp.where(routed, rank_ref[pl.ds(e, 1), :], -1.0)
    hits = jnp.where(routed, 1.0, 0.0)
    n = jnp.sum(hits[:, subs[0]]).astype(jnp.int32)
    for sl in subs[1:]:
        n = jnp.maximum(n, jnp.sum(hits[:, sl]).astype(jnp.int32))

    def chunk(base, ch):
        slot = (lax.broadcasted_iota(jnp.int32, (ch, sub), 0) + base).astype(F32)
        sels = [key[:, sl] == slot for sl in subs]
        gs = [jnp.where(sel, 1.0, 0.0).astype(BF16) for sel in sels]
        xg = jnp.concatenate([_dot(g, h_ref[sl, :]) for g, sl in zip(gs, subs)], axis=0).astype(BF16)
        w = jnp.concatenate([jnp.sum(jnp.where(sel, gate_e[:, sl], 0.0), axis=1, keepdims=True)
                             for sel, sl in zip(sels, subs)], axis=0)
        glu = jnp.minimum(_dot(xg, wg_ref[0]) + bg_ref[0], SWIGLU_LIMIT)
        lin = jnp.clip(_dot(xg, wl_ref[0]) + bl_ref[0], -SWIGLU_LIMIT, SWIGLU_LIMIT)
        act = glu * jax.nn.sigmoid(SWIGLU_ALPHA * glu) * (lin + 1.0)
        ys = ((_dot(act.astype(BF16), wd_ref[0]) + bd_ref[0]) * w).astype(BF16)
        for s, (g, sl) in enumerate(zip(gs, subs)):
            y_ref[sl, :] += lax.dot_general(g, ys[s * ch:(s + 1) * ch], (((0,), (0,)), ((), ())),
                                            preferred_element_type=F32)

    @pl.when(n > 0)
    def _():
        chunk(0, MOE_CHUNK)

    def tail(c, carry):
        chunk(MOE_CHUNK + c * MOE_TAIL, MOE_TAIL)
        return carry

    lax.fori_loop(0, (jnp.maximum(n - MOE_CHUNK, 0) + MOE_TAIL - 1) // MOE_TAIL, tail, 0)

    @pl.when(e == pl.num_programs(1) - 1)
    def _():
        x3 = x2_ref[...] + y_ref[...]
        ms = jnp.mean(x3 * x3, axis=-1, keepdims=True)
        y_ref[...] = x3 * lax.rsqrt(ms + EPS) * gf_ref[...]


def _moe(h2, gates_t, x2, w_glu, w_lin, b_glu, b_lin, w_down, b_down, g_final, sub, nb):
    rows, d = x2.shape
    dff = w_glu.shape[2]
    tile = sub * nb
    once = dict(pipeline_mode=pl.Buffered(1))
    row = lambda w, **kw: pl.BlockSpec((tile, w), lambda i, e: (i, 0), **kw)
    exp = lambda a, b: pl.BlockSpec((1, a, b), lambda i, e: (e, 0, 0))
    r = lax.broadcasted_iota(jnp.int32, (sub, sub), 0)
    c = lax.broadcasted_iota(jnp.int32, (sub, sub), 1)
    tri = jnp.where(r < c, 1.0, 0.0).astype(BF16)
    return pl.pallas_call(
        functools.partial(_moe_kernel, nb=nb),
        grid=(rows // tile, N_EXPERTS),
        in_specs=[row(d, **once), pl.BlockSpec((N_EXPERTS, tile), lambda i, e: (0, i), **once), row(d, **once),
                  pl.BlockSpec((sub, sub), lambda i, e: (0, 0), **once),
                  exp(d, dff), exp(d, dff), exp(1, dff), exp(1, dff),
                  exp(dff, d), exp(1, d), pl.BlockSpec((1, d), lambda i, e: (0, 0))],
        out_specs=row(d),
        out_shape=jax.ShapeDtypeStruct((rows, d), F32),
        scratch_shapes=[pltpu.VMEM((N_EXPERTS, tile), F32)],
        compiler_params=_cparams(("parallel", "arbitrary")),
        name="moe",
    )(h2, gates_t, x2, tri, w_glu, w_lin, b_glu, b_lin, w_down, b_down, g_final.reshape(1, d))


def _pick(n, pref):
    t = min(n, pref)
    assert n % t == 0, (n, pref)
    return t


def kernel(x_prompt, x_sample, cache_diff_k, cache_diff_v, cache_sb_k, cache_sb_v, meta_tokens, g_mix, w_in,
           lambda_q1, lambda_k1, lambda_q2, lambda_k2, g_subln, w_branch_a, w_branch_b, w_gate, b_gate, w_out,
           g_ffn, w_router, b_router, w_up, b_up, w_down, b_down, g_final):
    b, n, d = x_prompt.shape
    bs, ns, _ = x_sample.shape
    past = cache_diff_k.shape[1]
    rows_p, rows_s = b * n, bs * ns
    assert n % SB_SEG == 0 and past % SB_SEG == 0

    w_in_bf = w_in.astype(BF16)
    lams = jnp.stack([lambda_q1, lambda_k1, lambda_q2, lambda_k2]).astype(F32)

    tile_p = _pick(n, 512)
    per_seq = n // tile_p
    p32 = _proj(x_prompt.reshape(rows_p, d), g_mix, w_in_bf, _rotary_tables(N_META + jnp.arange(n)),
                tile_p, lambda i: i % per_seq)
    tile_s = _pick(rows_s, 512)
    assert tile_s % ns == 0
    s32 = _proj(x_sample.reshape(rows_s, d), g_mix, w_in_bf,
                _rotary_tables(N_META + past + (jnp.arange(tile_s) % ns)), tile_s, lambda i: 0)
    m32 = _proj(meta_tokens, g_mix, w_in_bf, _rotary_tables(jnp.arange(N_META)), N_META, lambda i: i)
    _, _, _, _, qa, qb, ka, kb, va, vb = p32
    _, _, _, _, sqa, sqb, ska, skb, sva, svb = s32
    _, _, _, _, _, _, mka, mkb, mva, mvb = m32

    tq = _pick(n, 512)
    oa_p = _diff_prompt(qa, ka, va, mka, mva, lams, g_subln, b, n, tq)
    ob_p = _sb_prompt(qb, kb, vb, mkb, mvb, b, n, tq)
    oa_s, ob_s = _sample_attention(
        sqa, sqb, ska, sva, skb, svb,
        jnp.transpose(cache_diff_k, (0, 2, 3, 4, 1)).reshape(bs, 512, past), cache_diff_v.reshape(bs, past, 512),
        jnp.transpose(cache_sb_k, (0, 2, 3, 1)).reshape(bs, 512, past),
        jnp.transpose(cache_sb_v, (0, 2, 3, 1)).reshape(bs, 512, past),
        mka, mva, mkb, mvb, lams, g_subln, 0, bs, ns, past)

    dff = w_down.shape[1]
    w_glu, w_lin = _deinterleave(w_up)
    w_down_bf = _to_bf16(w_down)
    b_glu = b_up[:, 0::2].reshape(N_EXPERTS, 1, dff)
    b_lin = b_up[:, 1::2].reshape(N_EXPERTS, 1, dff)
    merge_w = (g_mix, w_gate.astype(BF16), b_gate, w_branch_a.astype(BF16), w_branch_b.astype(BF16),
               w_out.astype(BF16), g_ffn, w_router, b_router)

    def ffn(x_rows, oa, ob, tile):
        x2, h2, gates_t = _merge(x_rows, oa, ob, *merge_w, tile)
        sub = _pick(x_rows.shape[0], MOE_SUB)
        nb = 2 if x_rows.shape[0] % (2 * sub) == 0 else 1
        return _moe(h2, gates_t, x2, w_glu, w_lin, b_glu, b_lin, w_down_bf, b_down.reshape(N_EXPERTS, 1, d),
                    g_final, sub, nb)

    y_p = ffn(x_prompt.reshape(rows_p, d), oa_p, ob_p, tile_p)
    y_s = ffn(x_sample.reshape(rows_s, d), oa_s, ob_s, tile_s)

    def with_meta(meta_rows, rows32, shape):
        full = jnp.concatenate([jnp.broadcast_to(meta_rows[None], (b, N_META, 512)), rows32.reshape(b, n, 512)], axis=1)
        return full.reshape((b, N_META + n) + shape)

    shapes = ((HA, 2, DHA), (HB, DHB), (HA, DVA), (HB, DHB))
    kv_p = [with_meta(m32[j], p32[j], shapes[j]) for j in range(4)]
    kv_s = [s32[j].reshape((bs, ns) + shapes[j]) for j in range(4)]
    return (y_p.reshape(b, n, d), y_s.reshape(bs, ns, d), kv_p[0], kv_p[2], kv_p[1], kv_p[3],
            kv_s[0], kv_s[2], kv_s[1], kv_s[3])
```
